```python
import math
import jax, jax.numpy as jnp
from jax import lax
import numpy as np

D_MODEL = 4096
BATCH = 4
SEQ = 2048
DEPTH = 1
DEC_BATCH = 128
DEC_SEQ = 8
PAST_LEN = 16384
PAGE_SIZE = 128

POOL_WIDTH = D_MODEL // 4
POOL_WINDOWS = (2, 4, 8, 16)
POOL_GROUPS = len(POOL_WINDOWS)
POOL_GROUP_WIDTH = POOL_WIDTH // POOL_GROUPS
POOL_BUF = max(POOL_WINDOWS) - 1
DN_WIDTH = D_MODEL - POOL_WIDTH
DN_HEAD_DIM = 128
DN_HEADS = DN_WIDTH // DN_HEAD_DIM
CONV_WIDTH = 4
CONV_CH = 3 * DN_WIDTH
PROJ_OUT = POOL_WIDTH + 4 * DN_WIDTH + 2 * DN_HEADS
D_FF = -(-8 * D_MODEL // (3 * 256)) * 256
PLE_DIM = 256
CHUNK = 64
LN_EPS = 1e-5
RMS_EPS = 1e-6
L2_EPS = 1e-6
DN_ALPHA = (2.0 * DEPTH) ** 0.25
DN_BETA = (8.0 * DEPTH) ** -0.25

kernel_name = 'hymba_pool_gdn_deepnorm_step'


def _layer_norm(x, g, b):
    xf = x.astype(jnp.float32)
    mu = jnp.mean(xf, -1, keepdims=True)
    var = jnp.mean(jnp.square(xf - mu), -1, keepdims=True)
    y = (xf - mu) * lax.rsqrt(var + LN_EPS) * g.astype(jnp.float32) + b.astype(jnp.float32)
    return y.astype(x.dtype)


def _l2norm(x):
    return x * lax.rsqrt(jnp.sum(x * x, -1, keepdims=True) + L2_EPS)


def _pool_mixer(u, buf, start, w_pool, pool_scale):
    B, L, _ = u.shape
    ext = jnp.concatenate([buf.astype(u.dtype), u], axis=1)
    cs = jnp.cumsum(ext.astype(jnp.float32), axis=1)
    cs = jnp.concatenate([jnp.zeros((B, 1, POOL_WIDTH), jnp.float32), cs], axis=1)
    pos = start + jnp.arange(L)
    uf = u.astype(jnp.float32)
    outs = []
    for gi, w in enumerate(POOL_WINDOWS):
        lo, hi = gi * POOL_GROUP_WIDTH, (gi + 1) * POOL_GROUP_WIDTH
        s_end = cs[:, POOL_BUF + 1:, lo:hi]
        s_begin = cs[:, POOL_BUF + 1 - w:POOL_BUF + 1 - w + L, lo:hi]
        cnt = jnp.minimum(pos + 1, w).astype(jnp.float32)[None, :, None]
        outs.append((s_end - s_begin) / cnt - uf[..., lo:hi])
    d = jnp.stack(outs, axis=2).astype(u.dtype)
    y = jnp.einsum('blgc,gcd->blgd', d, w_pool).reshape(B, L, POOL_WIDTH) * pool_scale
    return y, ext[:, -POOL_BUF:]


def _short_conv(xc, buf, w_conv):
    L = xc.shape[1]
    ext = jnp.concatenate([buf.astype(xc.dtype), xc], axis=1)
    out = ext[:, 0:L] * w_conv[0]
    for j in range(1, CONV_WIDTH):
        out = out + ext[:, j:j + L] * w_conv[j]
    return jax.nn.silu(out), ext[:, -(CONV_WIDTH - 1):]


def _gated_delta(q, k, v, g, beta, s0):
    B, L, H, DK = q.shape
    DV = v.shape[-1]
    C = math.gcd(L, CHUNK)
    N = L // C

    def blk(t):
        return jnp.moveaxis(t.reshape((B, N, C, H) + t.shape[3:]), 3, 1)

    q, k, v, g, beta = blk(q), blk(k), blk(v), blk(g), blk(beta)
    gc = jnp.cumsum(g, axis=-1)
    incl = jnp.tril(jnp.ones((C, C), bool))
    strict = jnp.tril(jnp.ones((C, C), bool), -1)
    diff = gc[..., :, None] - gc[..., None, :]
    decay = jnp.where(incl, jnp.exp(jnp.where(incl, diff, 0.0)), 0.0)
    kb = k * beta[..., None]
    vb = v * beta[..., None]
    a_mat = jnp.where(strict, jnp.einsum('bhnik,bhnjk->bhnij', kb, k) * decay, 0.0)
    eye = jnp.eye(C, dtype=jnp.float32)
    t_mat = lax.linalg.triangular_solve(a_mat + eye, jnp.broadcast_to(eye, a_mat.shape),
                                        left_side=True, lower=True)
    u_intra = jnp.einsum('bhnij,bhnjv->bhniv', t_mat, vb)
    w_intra = jnp.einsum('bhnij,bhnjk->bhnik', t_mat, kb * jnp.exp(gc)[..., None])
    qk = jnp.where(incl, jnp.einsum('bhnik,bhnjk->bhnij', q, k) * decay, 0.0)

    def step(s, xs):
        qn, kn, un, wn, gn, qkn = xs
        v_new = un - jnp.einsum('bhck,bhkv->bhcv', wn, s)
        o = (jnp.einsum('bhck,bhkv->bhcv', qn * jnp.exp(gn)[..., None], s)
             + jnp.einsum('bhij,bhjv->bhiv', qkn, v_new))
        g_last = gn[..., -1:]
        s = s * jnp.exp(g_last)[..., None] + jnp.einsum(
            'bhck,bhcv->bhkv', kn * jnp.exp(g_last - gn)[..., None], v_new)
        return s, o

    xs = tuple(jnp.moveaxis(t, 2, 0) for t in (q, k, u_intra, w_intra, gc, qk))
    s_fin, o = lax.scan(step, s0, xs)
    o = jnp.transpose(o, (1, 0, 3, 2, 4)).reshape(B, L, H, DV)
    return o, s_fin


def _mixer(x, pool_buf, conv_buf, s0, start, w_in, w_pool, pool_scale, w_conv, a_log, dt_bias,
           o_norm_g, w_out):
    B, L, _ = x.shape
    f32 = jnp.float32
    proj = jnp.einsum('bld,de->ble', x, w_in)
    o1 = POOL_WIDTH
    o2 = o1 + CONV_CH
    o3 = o2 + DN_WIDTH
    o4 = o3 + DN_HEADS
    u = proj[..., :o1]
    qkv = proj[..., o1:o2]
    z = proj[..., o2:o3]
    b_raw = proj[..., o3:o4]
    a_raw = proj[..., o4:]
    y_pool, new_pool = _pool_mixer(u, pool_buf, start, w_pool, pool_scale)
    qkv, new_conv = _short_conv(qkv, conv_buf, w_conv)
    qkv = qkv.astype(f32).reshape(B, L, 3, DN_HEADS, DN_HEAD_DIM)
    q = _l2norm(qkv[:, :, 0]) * DN_HEAD_DIM ** -0.5
    k = _l2norm(qkv[:, :, 1])
    v = qkv[:, :, 2]
    beta = jax.nn.sigmoid(b_raw.astype(f32))
    g = -jnp.exp(a_log.astype(f32)) * jax.nn.softplus(a_raw.astype(f32) + dt_bias.astype(f32))
    o, s_new = _gated_delta(q, k, v, g, beta, s0.astype(f32))
    o = o * lax.rsqrt(jnp.mean(o * o, -1, keepdims=True) + RMS_EPS) * o_norm_g.astype(f32)
    o = o * jax.nn.silu(z.astype(f32)).reshape(B, L, DN_HEADS, DN_HEAD_DIM)
    o = o.reshape(B, L, DN_WIDTH).astype(x.dtype)
    mixed = jnp.concatenate([y_pool.astype(x.dtype), o], axis=-1)
    return jnp.einsum('bld,de->ble', mixed, w_out), new_pool, new_conv, s_new


def _layer(x, p, pool_buf, conv_buf, s0, start, lw):
    (w_in, w_pool, pool_scale, w_conv, a_log, dt_bias, o_norm_g, w_out,
     ln1_g, ln1_b, w_gate_up, w_down, ln2_g, ln2_b, w_ple_gate, w_ple_proj) = lw
    mix, new_pool, new_conv, s_new = _mixer(x, pool_buf, conv_buf, s0, start, w_in, w_pool,
                                            pool_scale, w_conv, a_log, dt_bias, o_norm_g, w_out)
    h = _layer_norm(DN_ALPHA * x + mix, ln1_g, ln1_b)
    gu = jnp.einsum('bld,df->blf', h, w_gate_up)
    ff = jnp.einsum('blf,fd->bld', jax.nn.silu(gu[..., :D_FF]) * gu[..., D_FF:], w_down)
    h = _layer_norm(DN_ALPHA * h + ff, ln2_g, ln2_b)
    gate = jax.nn.sigmoid(jnp.einsum('bld,de->ble', h, w_ple_gate).astype(jnp.float32))
    e = jnp.einsum('blp,pd->bld', p.astype(x.dtype), w_ple_proj).astype(jnp.float32)
    y = h + (gate * e).astype(h.dtype)
    return y, new_pool, new_conv, s_new


def setup_inputs(seed: int = 0) -> dict:
    key = jax.random.key(seed)
    ks = jax.random.split(key, 24)
    f32 = jnp.float32

    def nrm(k, shape, s):
        return jax.random.normal(k, shape, f32) * s

    G = POOL_GROUP_WIDTH
    x_prompt = nrm(ks[0], (BATCH, SEQ, D_MODEL), 1.0)
    x_sample = nrm(ks[1], (DEC_BATCH, DEC_SEQ, D_MODEL), 1.0)
    state_pool = nrm(ks[2], (DEPTH, DEC_BATCH, POOL_BUF, POOL_WIDTH), 1.0)
    state_conv = nrm(ks[3], (DEPTH, DEC_BATCH, CONV_WIDTH - 1, CONV_CH), 1.0)
    state_delta = nrm(ks[4], (DEPTH, DEC_BATCH, DN_HEADS, DN_HEAD_DIM, DN_HEAD_DIM), 0.05)
    p_prompt = nrm(ks[5], (DEPTH, BATCH, SEQ, PLE_DIM), 1.0)
    p_sample = nrm(ks[6], (DEPTH, DEC_BATCH, DEC_SEQ, PLE_DIM), 1.0)
    w_in = nrm(ks[7], (DEPTH, D_MODEL, PROJ_OUT), D_MODEL ** -0.5)
    w_pool = nrm(ks[8], (DEPTH, POOL_GROUPS, G, G), G ** -0.5)
    pool_scale = 1.0 + nrm(ks[9], (DEPTH, POOL_WIDTH), 0.1)
    w_conv = nrm(ks[10], (DEPTH, CONV_WIDTH, CONV_CH), CONV_WIDTH ** -0.5)
    a_log = jnp.log(jax.random.uniform(ks[11], (DEPTH, DN_HEADS), f32, 1.0, 16.0))
    dt = jnp.exp(jax.random.uniform(ks[12], (DEPTH, DN_HEADS), f32, math.log(1e-3), math.log(1e-1)))
    dt_bias = dt + jnp.log(-jnp.expm1(-dt))
    o_norm_g = 1.0 + nrm(ks[13], (DEPTH, DN_HEAD_DIM), 0.1)
    w_out = nrm(ks[14], (DEPTH, D_MODEL, D_MODEL), D_MODEL ** -0.5 * DN_BETA)
    ln1_g = 1.0 + nrm(ks[15], (DEPTH, D_MODEL), 0.1)
    ln1_b = nrm(ks[16], (DEPTH, D_MODEL), 0.02)
    w_gate_up = nrm(ks[17], (DEPTH, D_MODEL, 2 * D_FF), D_MODEL ** -0.5)
    w_down = nrm(ks[18], (DEPTH, D_FF, D_MODEL), D_FF ** -0.5 * DN_BETA)
    ln2_g = 1.0 + nrm(ks[19], (DEPTH, D_MODEL), 0.1)
    ln2_b = nrm(ks[20], (DEPTH, D_MODEL), 0.02)
    w_ple_gate = nrm(ks[21], (DEPTH, D_MODEL, D_MODEL), D_MODEL ** -0.5)
    w_ple_proj = nrm(ks[22], (DEPTH, PLE_DIM, D_MODEL), PLE_DIM ** -0.5 * 0.5)
    return {'x_prompt': x_prompt, 'x_sample': x_sample, 'state_pool': state_pool,
            'state_conv': state_conv, 'state_delta': state_delta, 'p_prompt': p_prompt,
            'p_sample': p_sample, 'w_in': w_in, 'w_pool': w_pool, 'pool_scale': pool_scale,
            'w_conv': w_conv, 'a_log': a_log, 'dt_bias': dt_bias, 'o_norm_g': o_norm_g,
            'w_out': w_out, 'ln1_g': ln1_g, 'ln1_b': ln1_b, 'w_gate_up': w_gate_up,
            'w_down': w_down, 'ln2_g': ln2_g, 'ln2_b': ln2_b, 'w_ple_gate': w_ple_gate,
            'w_ple_proj': w_ple_proj}


def reference(x_prompt, x_sample, state_pool, state_conv, state_delta, p_prompt, p_sample,
              w_in, w_pool, pool_scale, w_conv, a_log, dt_bias, o_norm_g, w_out,
              ln1_g, ln1_b, w_gate_up, w_down, ln2_g, ln2_b, w_ple_gate, w_ple_proj):
    B = x_prompt.shape[0]
    yp, ys = x_prompt, x_sample
    pool_p, conv_p, delta_p, pool_s, conv_s, delta_s = [], [], [], [], [], []
    for i in range(DEPTH):
        lw = (w_in[i], w_pool[i], pool_scale[i], w_conv[i], a_log[i], dt_bias[i], o_norm_g[i],
              w_out[i], ln1_g[i], ln1_b[i], w_gate_up[i], w_down[i], ln2_g[i], ln2_b[i],
              w_ple_gate[i], w_ple_proj[i])
        pool0 = jnp.zeros((B, POOL_BUF, POOL_WIDTH), x_prompt.dtype)
        conv0 = jnp.zeros((B, CONV_WIDTH - 1, CONV_CH), x_prompt.dtype)
        s0 = jnp.zeros((B, DN_HEADS, DN_HEAD_DIM, DN_HEAD_DIM), jnp.float32)
        yp, npl, ncv, nst = _layer(yp, p_prompt[i], pool0, conv0, s0, 0, lw)
        ys, spl, scv, sst = _layer(ys, p_sample[i], state_pool[i], state_conv[i], state_delta[i],
                                   PAST_LEN, lw)
        pool_p.append(npl)
        conv_p.append(ncv)
        delta_p.append(nst)
        pool_s.append(spl)
        conv_s.append(scv)
        delta_s.append(sst)
    return (yp, ys, jnp.stack(pool_p), jnp.stack(conv_p), jnp.stack(delta_p),
            jnp.stack(pool_s), jnp.stack(conv_s), jnp.stack(delta_s))
```

```python
import functools
import math

import jax
import jax.numpy as jnp
from jax import lax
from jax.experimental import pallas as pl
from jax.experimental.pallas import tpu as pltpu

F32 = jnp.float32
BF16 = jnp.bfloat16

PAST_LEN = 16384
POOL_WINDOWS = (2, 4, 8, 16)
POOL_HALO = 16
CONV_HALO = 8
PROMPT_CHUNK = 64
LN_EPS = 1e-5
RMS_EPS = 1e-6
L2_EPS = 1e-6

LANE = 128
SUBLANE = 8
VMEM_BUDGET = 56 * 1024 * 1024


def _vmem_limit(block_bytes, scratch_bytes=0):
    est = 2 * sum(block_bytes) + scratch_bytes + (12 << 20)
    return int(min(est, VMEM_BUDGET))


def _nbytes(shape, dtype):
    return math.prod(shape) * jnp.dtype(dtype).itemsize


def _pick(n, candidates):
    for c in candidates:
        if n % c == 0:
            return c
    raise ValueError(f"no tile for {n} among {candidates}")


def _params(n_axes, block_bytes, scratch_bytes=0):
    return pltpu.CompilerParams(dimension_semantics=("arbitrary",) * n_axes,
                                vmem_limit_bytes=_vmem_limit(block_bytes, scratch_bytes))


def _dot(a, b):
    return jnp.dot(a, b, preferred_element_type=F32)


def _mm_kernel(x_ref, w_ref, o_ref):
    o_ref[...] = _dot(x_ref[...], w_ref[...]).astype(o_ref.dtype)


def _matmul(x, w, *, tm, tn, out_dtype):
    T, K = x.shape
    N = w.shape[1]
    blocks = [_nbytes((tm, K), x.dtype), _nbytes((K, tn), w.dtype), _nbytes((tm, tn), out_dtype)]
    return pl.pallas_call(
        _mm_kernel,
        grid=(T // tm, N // tn),
        in_specs=[pl.BlockSpec((tm, K), lambda i, j: (i, 0)),
                  pl.BlockSpec((K, tn), lambda i, j: (0, j))],
        out_specs=pl.BlockSpec((tm, tn), lambda i, j: (i, j)),
        out_shape=jax.ShapeDtypeStruct((T, N), out_dtype),
        compiler_params=_params(2, blocks, _nbytes((tm, tn), F32)),
        name="matmul",
    )(x, w)


def _split_row_specs(n_a, n_j, tm, tn):
    spec_a = pl.BlockSpec((tm, tn), lambda i, j: (jnp.minimum(i, n_a - 1), jnp.where(i < n_a, j, n_j - 1)))
    spec_b = pl.BlockSpec((tm, tn), lambda i, j: (jnp.maximum(i - n_a, 0), jnp.where(i < n_a, 0, j)))
    return spec_a, spec_b


def _mm_resid_kernel(x_ref, w_ref, *rest, alpha, n_a):
    res_refs, o_ref = rest[:-1], rest[-1]
    acc = _dot(x_ref[...], w_ref[...])
    if len(res_refs) == 1:
        res = res_refs[0][...]
    else:
        res = jnp.where(pl.program_id(0) < n_a, res_refs[0][...], res_refs[1][...])
    o_ref[...] = alpha * res + acc


def _matmul_resid(x, w, residuals, alpha, *, tm, tn):
    T, K = x.shape
    N = w.shape[1]
    n_j = N // tn
    n_a = residuals[0].shape[0] // tm
    if len(residuals) == 1:
        res_specs = [pl.BlockSpec((tm, tn), lambda i, j: (i, j))]
    else:
        res_specs = list(_split_row_specs(n_a, n_j, tm, tn))
    blocks = [_nbytes((tm, K), x.dtype), _nbytes((K, tn), w.dtype)] + [_nbytes((tm, tn), F32)] * (len(residuals) + 1)
    return pl.pallas_call(
        functools.partial(_mm_resid_kernel, alpha=alpha, n_a=n_a),
        grid=(T // tm, n_j),
        in_specs=[pl.BlockSpec((tm, K), lambda i, j: (i, 0)),
                  pl.BlockSpec((K, tn), lambda i, j: (0, j))] + res_specs,
        out_specs=pl.BlockSpec((tm, tn), lambda i, j: (i, j)),
        out_shape=jax.ShapeDtypeStruct((T, N), F32),
        compiler_params=_params(2, blocks, _nbytes((tm, tn), F32)),
        name="matmul_resid",
    )(x, w, *residuals)


def _ln_kernel(s_ref, g_ref, b_ref, h_ref, hb_ref):
    s = s_ref[...]
    mu = jnp.mean(s, -1, keepdims=True)
    c = s - mu
    var = jnp.mean(c * c, -1, keepdims=True)
    y = c * lax.rsqrt(var + LN_EPS) * g_ref[...] + b_ref[...]
    h_ref[...] = y
    hb_ref[...] = y.astype(BF16)


def _layer_norm(s, g, b, *, tm):
    T, D = s.shape
    blocks = [_nbytes((tm, D), F32)] * 2 + [_nbytes((tm, D), BF16), 2 * _nbytes((1, D), F32)]
    row = pl.BlockSpec((tm, D), lambda i: (i, 0))
    vec = pl.BlockSpec((1, D), lambda i: (0, 0))
    return pl.pallas_call(
        _ln_kernel,
        grid=(T // tm,),
        in_specs=[row, vec, vec],
        out_specs=[row, row],
        out_shape=[jax.ShapeDtypeStruct((T, D), F32), jax.ShapeDtypeStruct((T, D), BF16)],
        compiler_params=_params(1, blocks, 2 * _nbytes((tm, D), F32)),
        name="layer_norm",
    )(s, g.reshape(1, D), b.reshape(1, D))


def _swiglu_kernel(h_ref, wg_ref, wu_ref, o_ref):
    h = h_ref[...]
    gate = _dot(h, wg_ref[...])
    up = _dot(h, wu_ref[...])
    o_ref[...] = (gate * jax.nn.sigmoid(gate) * up).astype(o_ref.dtype)


def _swiglu(h, w_gate_up, d_ff, *, tm, tn):
    T, K = h.shape
    n_j = d_ff // tn
    blocks = [_nbytes((tm, K), BF16), 2 * _nbytes((K, tn), BF16), _nbytes((tm, tn), BF16)]
    return pl.pallas_call(
        _swiglu_kernel,
        grid=(T // tm, n_j),
        in_specs=[pl.BlockSpec((tm, K), lambda i, j: (i, 0)),
                  pl.BlockSpec((K, tn), lambda i, j: (0, j)),
                  pl.BlockSpec((K, tn), lambda i, j: (0, j + n_j))],
        out_specs=pl.BlockSpec((tm, tn), lambda i, j: (i, j)),
        out_shape=jax.ShapeDtypeStruct((T, d_ff), BF16),
        compiler_params=_params(2, blocks, 3 * _nbytes((tm, tn), F32)),
        name="swiglu",
    )(h, w_gate_up, w_gate_up)


def _ple_kernel(hb_ref, wg_ref, p_ref, wp_ref, h_ref, ya_ref, yb_ref, *, n_a):
    gate = jax.nn.sigmoid(_dot(hb_ref[...], wg_ref[...]))
    e = _dot(p_ref[...], wp_ref[...])
    y = h_ref[...] + gate * e
    i = pl.program_id(0)

    @pl.when(i < n_a)
    def _():
        ya_ref[...] = y

    @pl.when(i >= n_a)
    def _():
        yb_ref[...] = y


def _ple(hb, h, p, w_gate, w_proj, rows_a, *, tm, tn):
    T, D = h.shape
    P = p.shape[1]
    n_j = D // tn
    n_a = rows_a // tm
    spec_a, spec_b = _split_row_specs(n_a, n_j, tm, tn)
    blocks = [_nbytes((tm, D), BF16), _nbytes((D, tn), BF16), _nbytes((tm, P), BF16), _nbytes((P, tn), BF16)]
    blocks += [_nbytes((tm, tn), F32)] * 3
    return pl.pallas_call(
        functools.partial(_ple_kernel, n_a=n_a),
        grid=(T // tm, n_j),
        in_specs=[pl.BlockSpec((tm, D), lambda i, j: (i, 0)),
                  pl.BlockSpec((D, tn), lambda i, j: (0, j)),
                  pl.BlockSpec((tm, P), lambda i, j: (i, 0)),
                  pl.BlockSpec((P, tn), lambda i, j: (0, j)),
                  pl.BlockSpec((tm, tn), lambda i, j: (i, j))],
        out_specs=[spec_a, spec_b],
        out_shape=[jax.ShapeDtypeStruct((rows_a, D), F32), jax.ShapeDtypeStruct((T - rows_a, D), F32)],
        compiler_params=_params(2, blocks, 3 * _nbytes((tm, tn), F32)),
        name="ple",
    )(hb, w_gate, p, w_proj, h)


def _pool_kernel(u_ref, halo_ref, w_ref, scale_ref, o_ref, d_ref, *, bb, L, tl, start):
    P = u_ref.shape[1]
    G = len(POOL_WINDOWS)
    gw = P // G
    n_sub = L // tl
    row = lax.broadcasted_iota(jnp.int32, (tl, 1), 0)

    def sub_tile(bi, t, tail):
        r0 = pl.multiple_of(bi * L + t * tl, SUBLANE)
        x = u_ref[pl.ds(r0, tl), :]
        ext = jnp.concatenate([tail, x], axis=0)
        pos = start + t * tl + row
        for gi, win in enumerate(POOL_WINDOWS):
            cols = slice(gi * gw, (gi + 1) * gw)
            s = ext[:, cols]
            shift = 1
            while shift < win:
                s = s + pltpu.roll(s, shift, axis=0)
                shift *= 2
            cnt = jnp.minimum(pos + 1, win).astype(F32)
            d = s[POOL_HALO:] / cnt - x[:, cols]
            d_ref[pl.ds(r0, tl), cols] = d
        return ext[tl:]

    def seq(bi, carry):
        tail0 = halo_ref[bi]
        if n_sub == 1:
            sub_tile(bi, 0, tail0)
        else:
            lax.fori_loop(0, n_sub, lambda t, tail: sub_tile(bi, t, tail), tail0)
        return carry

    lax.fori_loop(0, bb, seq, 0)
    for gi in range(G):
        cols = slice(gi * gw, (gi + 1) * gw)
        y = _dot(d_ref[:, cols].astype(BF16), w_ref[gi])
        o_ref[:, cols] = (y * scale_ref[:, cols]).astype(o_ref.dtype)


def _pool_mixer(proj, row0, halo, w_pool, pool_scale, *, bb, L, start):
    B = halo.shape[0]
    P = halo.shape[2]
    G, gw, _ = w_pool.shape
    rows = bb * L
    tl = min(L, 256)
    blocks = [_nbytes((rows, P), F32), _nbytes((bb, POOL_HALO, P), F32), _nbytes(w_pool.shape, BF16),
              _nbytes((1, P), F32), _nbytes((rows, P), BF16)]
    blk0 = row0 // rows
    return pl.pallas_call(
        functools.partial(_pool_kernel, bb=bb, L=L, tl=tl, start=start),
        grid=(B // bb,),
        in_specs=[pl.BlockSpec((rows, P), lambda i: (blk0 + i, 0)),
                  pl.BlockSpec((bb, POOL_HALO, P), lambda i: (i, 0, 0)),
                  pl.BlockSpec((G, gw, gw), lambda i: (0, 0, 0)),
                  pl.BlockSpec((1, P), lambda i: (0, 0))],
        out_specs=pl.BlockSpec((rows, P), lambda i: (i, 0)),
        out_shape=jax.ShapeDtypeStruct((B * L, P), BF16),
        scratch_shapes=[pltpu.VMEM((rows, P), F32)],
        compiler_params=_params(1, blocks, _nbytes((rows, P), F32) + 4 * _nbytes((tl + POOL_HALO, P), F32)),
        name="pool_mixer",
    )(proj, halo, w_pool, pool_scale.reshape(1, P))


def _gates_kernel(t_ref, alog_ref, dtb_ref, beta_ref, gc_ref, *, C):
    t = t_ref[...]
    beta_ref[...] = jax.nn.sigmoid(t)
    x = t + dtb_ref[...]
    softplus = jnp.maximum(x, 0.0) + jnp.log1p(jnp.exp(-jnp.abs(x)))
    g = -jnp.exp(alog_ref[...]) * softplus
    pos = lax.broadcasted_iota(jnp.int32, g.shape, 0) % C
    shift = 1
    while shift < C:
        g = g + jnp.where(pos >= shift, pltpu.roll(g, shift, axis=0), 0.0)
        shift *= 2
    gc_ref[...] = g


def _gates(proj, row0, rows, tail_blk, alog_row, dtb_row, *, C, tm):
    spec = pl.BlockSpec((tm, LANE), lambda i: (i, 0))
    vec = pl.BlockSpec((1, LANE), lambda i: (0, 0))
    blk0 = row0 // tm
    return pl.pallas_call(
        functools.partial(_gates_kernel, C=C),
        grid=(rows // tm,),
        in_specs=[pl.BlockSpec((tm, LANE), lambda i: (blk0 + i, tail_blk)), vec, vec],
        out_specs=[spec, spec],
        out_shape=[jax.ShapeDtypeStruct((rows, LANE), F32)] * 2,
        compiler_params=_params(1, [4 * _nbytes((tm, LANE), F32)], 8 * _nbytes((tm, LANE), F32)),
        name="gates",
    )(proj, alog_row, dtb_row)


def _delta_kernel(*refs, bb, hb, L, C, has_state):
    (q_ref, k_ref, v_ref, z_ref, hq_ref, hk_ref, hv_ref, wq_ref, wk_ref, wv_ref,
     gch_ref, bth_ref, cf_ref) = refs[:13]
    rest = refs[13:]
    if has_state:
        s0_ref, rest = rest[0], rest[1:]
    ong_ref, o_ref, s_ref = rest
    D = LANE
    N = L // C
    n_fac = max(int(math.log2(C)) - 1, 0)
    ri = lax.broadcasted_iota(jnp.int32, (C, C), 0)
    ci = lax.broadcasted_iota(jnp.int32, (C, C), 1)
    incl, strict, eye = ri >= ci, ri > ci, ri == ci
    ong = ong_ref[...]

    def conv_silu(x, tail, w_ref, cols):
        ext = jnp.concatenate([tail, x], axis=0)
        y = x * w_ref[3:4, cols]
        for j in range(3):
            y = y + pltpu.roll(ext, 3 - j, axis=0)[CONV_HALO:] * w_ref[j:j + 1, cols]
        return y * jax.nn.sigmoid(y)

    def l2norm(x):
        return x * lax.rsqrt(jnp.sum(x * x, -1, keepdims=True) + L2_EPS)

    def head_chunk(bi, n, hi, tails):
        cols = slice(hi * D, (hi + 1) * D)
        r0 = pl.multiple_of(bi * L + n * C, SUBLANE)
        xq, xk, xv = (r[pl.ds(r0, C), cols] for r in (q_ref, k_ref, v_ref))
        q = l2norm(conv_silu(xq, tails[0], wq_ref, cols)) * (D ** -0.5)
        k = l2norm(conv_silu(xk, tails[1], wk_ref, cols))
        v = conv_silu(xv, tails[2], wv_ref, cols)
        new_tails = tuple(x[C - CONV_HALO:] for x in (xq, xk, xv))

        cf = cf_ref[0, pl.ds(r0, C), :]
        bcol = cf[:, hi:hi + 1]
        gcol = cf[:, LANE // 2 + hi:LANE // 2 + hi + 1]
        grow = gch_ref[bi, hi, pl.ds(n, 1), :]
        brow = bth_ref[bi, hi, pl.ds(n, 1), :]
        glast = grow[:, C - 1:C]

        decay = jnp.where(incl, jnp.exp(jnp.where(incl, gcol - grow, 0.0)), 0.0)
        qk2 = lax.dot_general(jnp.concatenate([q, k], axis=0), k, (((1,), (1,)), ((), ())),
                              preferred_element_type=F32)
        a = jnp.where(strict, qk2[C:] * decay * bcol, 0.0)
        p = jnp.where(eye, 1.0, 0.0) - a
        if n_fac > 0:
            sq = _dot(a, a)
            for f in range(n_fac):
                if f < n_fac - 1:
                    r = _dot(jnp.concatenate([p, sq], axis=0), sq)
                    p, sq = p + r[:C], r[C:]
                else:
                    p = p + _dot(p, sq)

        s_old = s_ref[bi, hi]
        kqs = _dot(jnp.concatenate([k, q], axis=0), s_old)
        tb = p * brow
        top = jnp.concatenate([tb, -(tb * jnp.exp(grow))], axis=1)
        lhs = jnp.concatenate([top, top * jnp.exp(glast - gcol)], axis=0)
        vn = _dot(lhs, jnp.concatenate([v, kqs[:C]], axis=0))
        lhs_o = jnp.concatenate([jnp.where(eye, jnp.exp(gcol), 0.0),
                                 jnp.where(incl, qk2[:C] * decay, 0.0)], axis=1)
        o = _dot(lhs_o, jnp.concatenate([kqs[C:], vn[:C]], axis=0))
        s_ref[bi, hi] = s_old * jnp.exp(glast) + lax.dot_general(
            k, vn[C:], (((0,), (0,)), ((), ())), preferred_element_type=F32)

        o = o * lax.rsqrt(jnp.mean(o * o, -1, keepdims=True) + RMS_EPS) * ong
        z = z_ref[pl.ds(r0, C), cols]
        o_ref[pl.ds(r0, C), cols] = (o * (z * jax.nn.sigmoid(z))).astype(o_ref.dtype)
        return new_tails

    def seq(bi, carry):
        tails = []
        for hi in range(hb):
            cols = slice(hi * D, (hi + 1) * D)
            s_ref[bi, hi] = s0_ref[bi, hi] if has_state else jnp.zeros((D, D), F32)
            tails.append(tuple(r[bi, :, cols] for r in (hq_ref, hk_ref, hv_ref)))

        def chunk(n, tails):
            return tuple(head_chunk(bi, n, hi, tails[hi]) for hi in range(hb))

        if N == 1:
            chunk(0, tuple(tails))
        else:
            lax.fori_loop(0, N, chunk, tuple(tails))
        return carry

    if bb == 1:
        seq(0, 0)
    else:
        lax.fori_loop(0, bb, seq, 0)


def _delta_mixer(proj, row0, offs, conv_halo, w_conv8, gc_hm, beta_hm, cf, s0, o_norm_g,
                 *, B, L, C, bb, hb, out_dtype):
    H = gc_hm.shape[1]
    D = LANE
    W = hb * D
    rows = bb * L
    N = L // C
    blk0 = row0 // rows
    has_state = s0 is not None

    def col_spec(off):
        return pl.BlockSpec((rows, W), lambda i, j, o=off // W: (blk0 + i, o + j))

    def halo_spec(g):
        return pl.BlockSpec((bb, CONV_HALO, W), lambda i, j, o=g * (H // hb): (i, 0, o + j))

    def w_spec(g):
        return pl.BlockSpec((CONV_HALO, W), lambda i, j, o=g * (H // hb): (0, o + j))

    hm_spec = pl.BlockSpec((bb, hb, N, C), lambda i, j: (i, j, 0, 0))
    state_spec = pl.BlockSpec((bb, hb, D, D), lambda i, j: (i, j, 0, 0))
    in_specs = [col_spec(offs[0]), col_spec(offs[1]), col_spec(offs[2]), col_spec(offs[3]),
                halo_spec(0), halo_spec(1), halo_spec(2), w_spec(0), w_spec(1), w_spec(2),
                hm_spec, hm_spec, pl.BlockSpec((1, rows, LANE), lambda i, j: (j, i, 0))]
    args = [proj, proj, proj, proj, conv_halo, conv_halo, conv_halo, w_conv8, w_conv8, w_conv8,
            gc_hm, beta_hm, cf]
    if has_state:
        in_specs.append(state_spec)
        args.append(s0)
    in_specs.append(pl.BlockSpec((1, D), lambda i, j: (0, 0)))
    args.append(o_norm_g.reshape(1, D))
    blocks = [4 * _nbytes((rows, W), F32), 3 * _nbytes((bb, CONV_HALO, W), F32), 3 * _nbytes((CONV_HALO, W), F32),
              2 * _nbytes((bb, hb, max(N, SUBLANE), LANE), F32), _nbytes((rows, LANE), F32),
              (2 if has_state else 1) * _nbytes((bb, hb, D, D), F32), _nbytes((rows, W), out_dtype)]
    return pl.pallas_call(
        functools.partial(_delta_kernel, bb=bb, hb=hb, L=L, C=C, has_state=has_state),
        grid=(B // bb, H // hb),
        in_specs=in_specs,
        out_specs=[pl.BlockSpec((rows, W), lambda i, j: (i, j)), state_spec],
        out_shape=[jax.ShapeDtypeStruct((B * L, H * D), out_dtype),
                   jax.ShapeDtypeStruct((B, H, D, D), F32)],
        compiler_params=_params(2, blocks),
        name="delta_mixer",
    )(*args)


def _head_major(tok, B, L, C, H, lane0):
    x = tok[:, lane0:lane0 + H].reshape(B, L // C, C, H)
    return jnp.transpose(x, (0, 3, 1, 2))


def _col_form(beta_tok, gc_tok, H, hb):
    T = beta_tok.shape[0]
    half = LANE // 2

    def part(tok, lane0):
        x = tok[:, lane0:lane0 + H].reshape(T, H // hb, hb)
        x = jnp.transpose(x, (1, 0, 2))
        return jnp.pad(x, ((0, 0), (0, 0), (0, half - hb)))

    return jnp.concatenate([part(beta_tok, 0), part(gc_tok, H)], axis=-1)


def _mixer_stream(proj, row0, B, L, C, start, pool_state, conv_state, delta_state, lw, dims, *, bb_pool, bb, hb,
                  out_dtype):
    P, DN, H, tail_blk, tm_gate = dims
    (w_pool_b, pool_scale, w_conv8, alog_row, dtb_row, o_norm_g) = lw
    pool_halo = jnp.pad(pool_state, ((0, 0), (POOL_HALO - pool_state.shape[1], 0), (0, 0)))
    y_pool = _pool_mixer(proj, row0, pool_halo, w_pool_b, pool_scale, bb=bb_pool, L=L, start=start)
    beta_tok, gc_tok = _gates(proj, row0, B * L, tail_blk, alog_row, dtb_row, C=C, tm=tm_gate)
    conv_halo = jnp.pad(conv_state, ((0, 0), (CONV_HALO - conv_state.shape[1], 0), (0, 0)))
    offs = (P, P + DN, P + 2 * DN, P + 3 * DN)
    o, s_new = _delta_mixer(proj, row0, offs, conv_halo, w_conv8,
                            _head_major(gc_tok, B, L, C, H, H), _head_major(beta_tok, B, L, C, H, 0),
                            _col_form(beta_tok, gc_tok, H, hb), delta_state, o_norm_g,
                            B=B, L=L, C=C, bb=bb, hb=hb, out_dtype=out_dtype)
    return jnp.concatenate([y_pool, o.astype(BF16)], axis=-1), s_new


def kernel(x_prompt, x_sample, state_pool, state_conv, state_delta, p_prompt, p_sample, w_in, w_pool, pool_scale, w_conv, a_log, dt_bias, o_norm_g, w_out, ln1_g, ln1_b, w_gate_up, w_down, ln2_g, ln2_b, w_ple_gate, w_ple_proj):
    depth = w_in.shape[0]
    B, L, D = x_prompt.shape
    Bs, Ls, _ = x_sample.shape
    P = w_pool.shape[1] * w_pool.shape[2]
    H = a_log.shape[1]
    DN = H * LANE
    d_ff = w_down.shape[1]
    n_proj = w_in.shape[2]
    Tp, Ts = B * L, Bs * Ls
    T = Tp + Ts
    alpha = (2.0 * depth) ** 0.25
    assert n_proj == P + 4 * DN + 2 * H and 2 * H <= LANE // 2 and (P + 4 * DN) % LANE == 0
    assert w_conv.shape[1] == 4 and D == P + DN and max(POOL_WINDOWS) - 1 == state_pool.shape[2]

    tm = _pick(math.gcd(Tp, Ts), (1024, 512, 256, 128))
    tm_ln = min(tm, 256)
    n_pad = -(-n_proj // LANE) * LANE
    tn_proj = _pick(n_pad, (640, 512, 384, 256, 128))
    tn_d = _pick(D, (512, 256, 128))
    tn_ff = _pick(d_ff, (256, 128))
    tm_down = min(tm, 512)
    Cp = math.gcd(L, PROMPT_CHUNK)
    Cs = math.gcd(Ls, PROMPT_CHUNK)
    hb_p = _pick(H, (2, 1))
    hb_s = _pick(H, (8, 4, 2, 1))
    bb_s = _pick(Bs, (8, 4, 2, 1))
    bb_pool_s = _pick(Bs, (16, 8, 4, 2, 1))
    dims = (P, DN, H, (P + 4 * DN) // LANE, None)

    xp2, xs2 = x_prompt.reshape(Tp, D), x_sample.reshape(Ts, D)
    y_rows = jnp.concatenate([xp2, xs2], axis=0).astype(BF16)
    res_rows = (xp2, xs2)
    outs = [[] for _ in range(6)]
    for li in range(depth):
        w_in_b = jnp.pad(w_in[li], ((0, 0), (0, n_pad - n_proj))).astype(BF16)
        w_conv8 = jnp.pad(w_conv[li], ((0, CONV_HALO - w_conv.shape[1]), (0, 0)))
        alog_row = jnp.pad(a_log[li], (H, LANE - 2 * H)).reshape(1, LANE)
        dtb_row = jnp.pad(dt_bias[li], (H, LANE - 2 * H)).reshape(1, LANE)
        lw = (w_pool[li].astype(BF16), pool_scale[li], w_conv8, alog_row, dtb_row, o_norm_g[li])

        proj = _matmul(y_rows, w_in_b, tm=tm, tn=tn_proj, out_dtype=F32)

        zeros_pool = jnp.zeros((B,) + state_pool.shape[2:], F32)
        zeros_conv = jnp.zeros((B,) + state_conv.shape[2:], F32)
        mixed_p, sp = _mixer_stream(proj, 0, B, L, Cp, 0, zeros_pool, zeros_conv, None, lw,
                                    dims[:4] + (_pick(Tp, (1024, 512, 256, 128, 64)),),
                                    bb_pool=1, bb=1, hb=hb_p, out_dtype=BF16)
        mixed_s, ss = _mixer_stream(proj, Tp, Bs, Ls, Cs, PAST_LEN, state_pool[li], state_conv[li], state_delta[li],
                                    lw, dims[:4] + (_pick(Ts, (1024, 512, 256, 128, 64)),),
                                    bb_pool=bb_pool_s, bb=bb_s, hb=hb_s, out_dtype=F32)
        mixed = jnp.concatenate([mixed_p, mixed_s], axis=0)

        u_p = proj[:Tp, :P].reshape(B, L, P)
        u_s = proj[Tp:, :P].reshape(Bs, Ls, P)
        c_p = proj[:Tp, P:P + 3 * DN].reshape(B, L, 3 * DN)
        c_s = proj[Tp:, P:P + 3 * DN].reshape(Bs, Ls, 3 * DN)
        n_pool, n_conv = state_pool.shape[2], state_conv.shape[2]
        outs[0].append(jnp.concatenate([zeros_pool, u_p], axis=1)[:, -n_pool:])
        outs[1].append(jnp.concatenate([zeros_conv, c_p], axis=1)[:, -n_conv:])
        outs[2].append(sp)
        outs[3].append(jnp.concatenate([state_pool[li], u_s], axis=1)[:, -n_pool:])
        outs[4].append(jnp.concatenate([state_conv[li], c_s], axis=1)[:, -n_conv:])
        outs[5].append(ss)

        s1 = _matmul_resid(mixed, w_out[li].astype(BF16), res_rows, alpha, tm=tm, tn=tn_d)
        h1, h1b = _layer_norm(s1, ln1_g[li], ln1_b[li], tm=tm_ln)
        act = _swiglu(h1b, w_gate_up[li].astype(BF16), d_ff, tm=tm, tn=tn_ff)
        s2 = _matmul_resid(act, w_down[li].astype(BF16), (h1,), alpha, tm=tm_down, tn=tn_ff)
        h2, h2b = _layer_norm(s2, ln2_g[li], ln2_b[li], tm=tm_ln)
        p_rows = jnp.concatenate([p_prompt[li].reshape(Tp, -1), p_sample[li].reshape(Ts, -1)], axis=0).astype(BF16)
        yp2, ys2 = _ple(h2b, h2, p_rows, w_ple_gate[li].astype(BF16), w_ple_proj[li].astype(BF16), Tp,
                        tm=tm, tn=tn_d)
        y_rows = jnp.concatenate([yp2, ys2], axis=0).astype(BF16)
        res_rows = (yp2, ys2)

    return (yp2.reshape(B, L, D), ys2.reshape(Bs, Ls, D),
            jnp.stack(outs[0]), jnp.stack(outs[1]), jnp.stack(outs[2]),
            jnp.stack(outs[3]), jnp.stack(outs[4]), jnp.stack(outs[5]))
```

```python
import functools
import math

import jax
import jax.numpy as jnp
from jax import lax
from jax.experimental import pallas as pl
from jax.experimental.pallas import tpu as pltpu

F32 = jnp.float32
BF16 = jnp.bfloat16

PAST_LEN = 16384
POOL_WINDOWS = (2, 4, 8, 16)
POOL_HALO = 16
CONV_HALO = 8
CHUNK = 64
LN_EPS = 1e-5
RMS_EPS = 1e-6
L2_EPS = 1e-6

LANE = 128
SUBLANE = 8
VMEM_BUDGET = 56 * 1024 * 1024

CF_BETA, CF_GC, CF_GLAST = 0, 32, 64


def _vmem_limit(block_bytes, scratch_bytes=0):
    est = 2 * sum(block_bytes) + scratch_bytes + (12 << 20)
    return int(min(est, VMEM_BUDGET))


def _nbytes(shape, dtype):
    return math.prod(shape) * jnp.dtype(dtype).itemsize


def _pick(n, candidates):
    for c in candidates:
        if n % c == 0:
            return c
    raise ValueError(f"no tile for {n} among {candidates}")


def _params(n_axes, block_bytes, scratch_bytes=0):
    return pltpu.CompilerParams(dimension_semantics=("arbitrary",) * n_axes,
                                vmem_limit_bytes=_vmem_limit(block_bytes, scratch_bytes))


def _dot(a, b):
    return jnp.dot(a, b, preferred_element_type=F32)


def _dot_t0(a, b):
    return lax.dot_general(a, b, (((0,), (0,)), ((), ())), preferred_element_type=F32)


def _mm_kernel(x_ref, w_ref, o_ref):
    o_ref[...] = _dot(x_ref[...], w_ref[...]).astype(o_ref.dtype)


def _matmul(x, w, n_cols, *, tm, tn, out_dtype):
    T, K = x.shape
    blocks = [_nbytes((tm, K), x.dtype), _nbytes((K, tn), w.dtype), _nbytes((tm, tn), out_dtype)]
    return pl.pallas_call(
        _mm_kernel,
        grid=(T // tm, n_cols // tn),
        in_specs=[pl.BlockSpec((tm, K), lambda i, j: (i, 0)),
                  pl.BlockSpec((K, tn), lambda i, j: (0, j))],
        out_specs=pl.BlockSpec((tm, tn), lambda i, j: (i, j)),
        out_shape=jax.ShapeDtypeStruct((T, n_cols), out_dtype),
        compiler_params=_params(2, blocks, _nbytes((tm, tn), F32)),
        name="matmul",
    )(x, w)


def _mm_resid_kernel(*refs, alpha, splits):
    x_refs, (w_ref, r_ref, o_ref) = refs[:len(splits)], refs[len(splits):]
    acc = alpha * r_ref[...]
    k0 = 0
    for x_ref, kw in zip(x_refs, splits):
        acc = acc + _dot(x_ref[...], w_ref[k0:k0 + kw, :])
        k0 += kw
    o_ref[...] = acc


def _matmul_resid(xs, w, resid, alpha, *, tm, tn):
    T = xs[0].shape[0]
    splits = tuple(x.shape[1] for x in xs)
    K, N = w.shape
    blocks = [_nbytes((tm, K), BF16), _nbytes((K, tn), BF16), 2 * _nbytes((tm, tn), F32)]
    return pl.pallas_call(
        functools.partial(_mm_resid_kernel, alpha=alpha, splits=splits),
        grid=(T // tm, N // tn),
        in_specs=[pl.BlockSpec((tm, kw), lambda i, j: (i, 0)) for kw in splits]
        + [pl.BlockSpec((K, tn), lambda i, j: (0, j)), pl.BlockSpec((tm, tn), lambda i, j: (i, j))],
        out_specs=pl.BlockSpec((tm, tn), lambda i, j: (i, j)),
        out_shape=jax.ShapeDtypeStruct((T, N), F32),
        compiler_params=_params(2, blocks, _nbytes((tm, tn), F32)),
        name="matmul_resid",
    )(*xs, w, resid)


def _ln_kernel(s_ref, g_ref, b_ref, h_ref, hb_ref):
    s = s_ref[...]
    mu = jnp.mean(s, -1, keepdims=True)
    c = s - mu
    var = jnp.mean(c * c, -1, keepdims=True)
    y = c * lax.rsqrt(var + LN_EPS) * g_ref[...] + b_ref[...]
    h_ref[...] = y
    hb_ref[...] = y.astype(BF16)


def _layer_norm(s, g, b, *, tm):
    T, D = s.shape
    blocks = [_nbytes((tm, D), F32)] * 2 + [_nbytes((tm, D), BF16), 2 * _nbytes((1, D), F32)]
    row = pl.BlockSpec((tm, D), lambda i: (i, 0))
    vec = pl.BlockSpec((1, D), lambda i: (0, 0))
    return pl.pallas_call(
        _ln_kernel,
        grid=(T // tm,),
        in_specs=[row, vec, vec],
        out_specs=[row, row],
        out_shape=[jax.ShapeDtypeStruct((T, D), F32), jax.ShapeDtypeStruct((T, D), BF16)],
        compiler_params=_params(1, blocks, 2 * _nbytes((tm, D), F32)),
        name="layer_norm",
    )(s, g.reshape(1, D), b.reshape(1, D))


def _swiglu_kernel(h_ref, wg_ref, wu_ref, o_ref):
    h = h_ref[...]
    gate = _dot(h, wg_ref[...])
    up = _dot(h, wu_ref[...])
    o_ref[...] = (gate * jax.nn.sigmoid(gate) * up).astype(o_ref.dtype)


def _swiglu(h, w_gate_up, d_ff, *, tm, tn):
    T, K = h.shape
    n_j = d_ff // tn
    blocks = [_nbytes((tm, K), BF16), 2 * _nbytes((K, tn), BF16), _nbytes((tm, tn), BF16)]
    return pl.pallas_call(
        _swiglu_kernel,
        grid=(T // tm, n_j),
        in_specs=[pl.BlockSpec((tm, K), lambda i, j: (i, 0)),
                  pl.BlockSpec((K, tn), lambda i, j: (0, j)),
                  pl.BlockSpec((K, tn), lambda i, j: (0, j + n_j))],
        out_specs=pl.BlockSpec((tm, tn), lambda i, j: (i, j)),
        out_shape=jax.ShapeDtypeStruct((T, d_ff), BF16),
        compiler_params=_params(2, blocks, 3 * _nbytes((tm, tn), F32)),
        name="swiglu",
    )(h, w_gate_up, w_gate_up)


def _ple_kernel(hb_ref, wg_ref, p_ref, wp_ref, h_ref, y_ref):
    gate = jax.nn.sigmoid(_dot(hb_ref[...], wg_ref[...]))
    e = _dot(p_ref[...], wp_ref[...])
    y_ref[...] = h_ref[...] + gate * e


def _ple(hb, h, p, w_gate, w_proj, *, tm, tn):
    T, D = h.shape
    P = p.shape[1]
    blocks = [_nbytes((tm, D), BF16), _nbytes((D, tn), BF16), _nbytes((tm, P), BF16), _nbytes((P, tn), BF16),
              2 * _nbytes((tm, tn), F32)]
    return pl.pallas_call(
        _ple_kernel,
        grid=(T // tm, D // tn),
        in_specs=[pl.BlockSpec((tm, D), lambda i, j: (i, 0)),
                  pl.BlockSpec((D, tn), lambda i, j: (0, j)),
                  pl.BlockSpec((tm, P), lambda i, j: (i, 0)),
                  pl.BlockSpec((P, tn), lambda i, j: (0, j)),
                  pl.BlockSpec((tm, tn), lambda i, j: (i, j))],
        out_specs=pl.BlockSpec((tm, tn), lambda i, j: (i, j)),
        out_shape=jax.ShapeDtypeStruct((T, D), F32),
        compiler_params=_params(2, blocks, 3 * _nbytes((tm, tn), F32)),
        name="ple",
    )(hb, w_gate, p, w_proj, h)


def _pool_kernel(u_ref, halo_ref, w_ref, scale_ref, o_ref, d_ref, *, bb, L, tl, start):
    P = u_ref.shape[1]
    G = len(POOL_WINDOWS)
    gw = P // G
    n_sub = L // tl
    row = lax.broadcasted_iota(jnp.int32, (tl, 1), 0)

    def sub_tile(bi, t, tail):
        r0 = pl.multiple_of(bi * L + t * tl, SUBLANE)
        x = u_ref[pl.ds(r0, tl), :]
        ext = jnp.concatenate([tail, x], axis=0)
        pos = start + t * tl + row
        for gi, win in enumerate(POOL_WINDOWS):
            cols = slice(gi * gw, (gi + 1) * gw)
            s = ext[:, cols]
            shift = 1
            while shift < win:
                s = s + pltpu.roll(s, shift, axis=0)
                shift *= 2
            cnt = jnp.minimum(pos + 1, win).astype(F32)
            d = s[POOL_HALO:] / cnt - x[:, cols]
            d_ref[pl.ds(r0, tl), cols] = d
        return ext[tl:]

    def seq(bi, carry):
        tail0 = halo_ref[bi]
        if n_sub == 1:
            sub_tile(bi, 0, tail0)
        else:
            lax.fori_loop(0, n_sub, lambda t, tail: sub_tile(bi, t, tail), tail0)
        return carry

    lax.fori_loop(0, bb, seq, 0)
    for gi in range(G):
        cols = slice(gi * gw, (gi + 1) * gw)
        y = _dot(d_ref[:, cols].astype(BF16), w_ref[gi])
        o_ref[:, cols] = (y * scale_ref[:, cols]).astype(o_ref.dtype)


def _pool_mixer(proj, halo, w_pool, pool_scale, *, bb, L, start):
    B, _, P = halo.shape
    G, gw, _ = w_pool.shape
    rows = bb * L
    tl = min(L, 256)
    blocks = [_nbytes((rows, P), F32), _nbytes((bb, POOL_HALO, P), F32), _nbytes(w_pool.shape, BF16),
              _nbytes((1, P), F32), _nbytes((rows, P), BF16)]
    return pl.pallas_call(
        functools.partial(_pool_kernel, bb=bb, L=L, tl=tl, start=start),
        grid=(B // bb,),
        in_specs=[pl.BlockSpec((rows, P), lambda i: (i, 0)),
                  pl.BlockSpec((bb, POOL_HALO, P), lambda i: (i, 0, 0)),
                  pl.BlockSpec((G, gw, gw), lambda i: (0, 0, 0)),
                  pl.BlockSpec((1, P), lambda i: (0, 0))],
        out_specs=pl.BlockSpec((rows, P), lambda i: (i, 0)),
        out_shape=jax.ShapeDtypeStruct((B * L, P), BF16),
        scratch_shapes=[pltpu.VMEM((rows, P), F32)],
        compiler_params=_params(1, blocks, _nbytes((rows, P), F32) + 4 * _nbytes((tl + POOL_HALO, P), F32)),
        name="pool_mixer",
    )(proj, halo, w_pool, pool_scale.reshape(1, P))


def _gates_kernel(x_ref, w_ref, alog_ref, dtb_ref, beta_ref, gc_ref, gl_ref, *, unit):
    t = _dot(x_ref[...], w_ref[...])
    beta_ref[...] = jax.nn.sigmoid(t)
    x = t + dtb_ref[...]
    softplus = jnp.maximum(x, 0.0) + jnp.log1p(jnp.exp(-jnp.abs(x)))
    g = -jnp.exp(alog_ref[...]) * softplus
    rows = g.shape[0]
    pos = lax.broadcasted_iota(jnp.int32, g.shape, 0) % unit
    shift = 1
    while shift < unit:
        g = g + jnp.where(pos >= shift, pltpu.roll(g, shift, axis=0), 0.0)
        shift *= 2
    gc_ref[...] = g
    shift = 1
    while shift < unit:
        g = jnp.where((pos & shift) == 0, pltpu.roll(g, rows - shift, axis=0), g)
        shift *= 2
    gl_ref[...] = g


def _gates(x, w_tail, alog_row, dtb_row, *, unit, tm):
    T, D = x.shape
    spec = pl.BlockSpec((tm, LANE), lambda i: (i, 0))
    vec = pl.BlockSpec((1, LANE), lambda i: (0, 0))
    blocks = [_nbytes((tm, D), BF16), _nbytes((D, LANE), BF16), 3 * _nbytes((tm, LANE), F32)]
    return pl.pallas_call(
        functools.partial(_gates_kernel, unit=unit),
        grid=(T // tm,),
        in_specs=[pl.BlockSpec((tm, D), lambda i: (i, 0)), pl.BlockSpec((D, LANE), lambda i: (0, 0)), vec, vec],
        out_specs=[spec, spec, spec],
        out_shape=[jax.ShapeDtypeStruct((T, LANE), F32)] * 3,
        compiler_params=_params(1, blocks, 8 * _nbytes((tm, LANE), F32)),
        name="gates",
    )(x, w_tail, alog_row, dtb_row)


def _zmap(fn, *lists):
    return [fn(*args) for args in zip(*lists)]


def _block_terms(qs, ks, vs, cfs, his, grows, brows, masks, n_fac):
    incl, strict, eye = masks
    C, D = qs[0].shape
    bcols = [cf[:, CF_BETA + hi:CF_BETA + hi + 1] for cf, hi in zip(cfs, his)]
    gcols = [cf[:, CF_GC + hi:CF_GC + hi + 1] for cf, hi in zip(cfs, his)]
    glcols = [cf[:, CF_GLAST + hi:CF_GLAST + hi + 1] for cf, hi in zip(cfs, his)]
    decays = _zmap(lambda gc, gr: jnp.where(incl, jnp.exp(jnp.where(incl, gc - gr, 0.0)), 0.0), gcols, grows)
    qk2s = _zmap(lambda q, k: lax.dot_general(jnp.concatenate([q, k], axis=0), k, (((1,), (1,)), ((), ())),
                                              preferred_element_type=F32), qs, ks)
    a_s = _zmap(lambda qk2, dc, bc: jnp.where(strict, qk2[C:] * dc * bc, 0.0), qk2s, decays, bcols)
    ps = [jnp.where(eye, 1.0, 0.0) - a for a in a_s]
    if n_fac > 0:
        sqs = [_dot(a, a) for a in a_s]
        for f in range(n_fac):
            if f < n_fac - 1:
                rs = _zmap(lambda p, sq: _dot(jnp.concatenate([p, sq], axis=0), sq), ps, sqs)
                ps = _zmap(lambda p, r: p + r[:C], ps, rs)
                sqs = [r[C:] for r in rs]
            else:
                ps = _zmap(lambda p, sq: p + _dot(p, sq), ps, sqs)
    tbs = _zmap(lambda p, br: p * br, ps, brows)
    w1s = _zmap(lambda tb, gr, k: _dot(tb * jnp.exp(gr), k), tbs, grows, ks)
    u1s = _zmap(_dot, tbs, vs)
    wus = _zmap(lambda w1, u1: jnp.concatenate([w1, u1], axis=1), w1s, u1s)
    tops = _zmap(lambda qk2, dc, wu: _dot(jnp.where(incl, qk2[:C] * dc, 0.0), wu), qk2s, decays, wus)
    q_effs = _zmap(lambda q, gc, top: q * jnp.exp(gc) - top[:, :D], qs, gcols, tops)
    k_decs = _zmap(lambda k, gl, gc: k * jnp.exp(gl - gc), ks, glcols, gcols)
    return wus, q_effs, [top[:, D:] for top in tops], k_decs


def _conv_silu(xs, shifteds, w_ref, colss):
    ys = _zmap(lambda x, cols: x * w_ref[3:4, cols], xs, colss)
    for s in (1, 2, 3):
        ys = _zmap(lambda y, sh, cols: y + sh(s) * w_ref[3 - s:4 - s, cols], ys, shifteds, colss)
    return [y * jax.nn.sigmoid(y) for y in ys]


def _l2norm(x):
    return x * lax.rsqrt(jnp.sum(x * x, -1, keepdims=True) + L2_EPS)


def _finish(o, z, ong):
    o = o * lax.rsqrt(jnp.mean(o * o, -1, keepdims=True) + RMS_EPS) * ong
    return o * (z * jax.nn.sigmoid(z))


def _masks(C, unit):
    ri = lax.broadcasted_iota(jnp.int32, (C, C), 0)
    ci = lax.broadcasted_iota(jnp.int32, (C, C), 1)
    same = (ri // unit) == (ci // unit)
    return same & (ri >= ci), same & (ri > ci), ri == ci


def _delta_seq_kernel(q_ref, k_ref, v_ref, z_ref, pq_ref, pk_ref, pv_ref, hq_ref, hk_ref, hv_ref,
                      wq_ref, wk_ref, wv_ref, gch_ref, bth_ref, cf_ref, ong_ref, o_ref, s_ref,
                      lhs_scr, n_scr, o0_scr, *, hb, L, unroll):
    C, D = CHUNK, LANE
    N = L // C
    masks = _masks(C, C)
    n_fac = int(math.log2(C)) - 1
    ong = ong_ref[...]
    first_tile = pl.program_id(2) == 0

    @pl.when(first_tile)
    def _():
        for hi in range(hb):
            s_ref[0, hi] = jnp.zeros((D, D), F32)

    def pass1(g, carry):
        items = [(g * unroll + u, hi) for u in range(unroll) for hi in range(hb)]
        ns = [n for n, _ in items]
        his = [hi for _, hi in items]
        colss = [slice(hi * D, (hi + 1) * D) for hi in his]
        r0s = [pl.multiple_of(n * C, C) for n in ns]
        rps = [pl.multiple_of(jnp.maximum(r0 - CONV_HALO, 0), SUBLANE) for r0 in r0s]

        def conv(x_ref, p_ref, h_ref, w_ref):
            xs = _zmap(lambda r0, cols: x_ref[pl.ds(r0, C), cols], r0s, colss)
            prevs = _zmap(lambda n, rp, cols: jnp.where(
                n > 0, x_ref[pl.ds(rp, CONV_HALO), cols],
                jnp.where(first_tile, h_ref[0, :, cols], p_ref[:, cols])), ns, rps, colss)
            exts = _zmap(lambda prev, x: jnp.concatenate([prev, x], axis=0), prevs, xs)
            shifteds = [lambda s, ext=ext: pltpu.roll(ext, s, axis=0)[CONV_HALO:] for ext in exts]
            return _conv_silu(xs, shifteds, w_ref, colss)

        qs = [_l2norm(y) * (D ** -0.5) for y in conv(q_ref, pq_ref, hq_ref, wq_ref)]
        ks = [_l2norm(y) for y in conv(k_ref, pk_ref, hk_ref, wk_ref)]
        vs = conv(v_ref, pv_ref, hv_ref, wv_ref)
        cfs = [cf_ref[0, pl.ds(r0, C), :] for r0 in r0s]
        grows = [gch_ref[0, hi, pl.ds(n, 1), :] for n, hi in items]
        brows = [bth_ref[0, hi, pl.ds(n, 1), :] for n, hi in items]
        wus, q_effs, o0s, k_decs = _block_terms(qs, ks, vs, cfs, his, grows, brows, masks, n_fac)
        mns = _zmap(_dot_t0, k_decs, wus)
        for (n, hi), mn, q_eff, o0 in zip(items, mns, q_effs, o0s):
            lhs_scr[hi, n, :D, :] = mn[:, :D]
            lhs_scr[hi, n, D:, :] = q_eff
            n_scr[hi, n] = mn[:, D:]
            o0_scr[hi, n] = o0
        return carry

    if N == unroll:
        pass1(0, 0)
    else:
        lax.fori_loop(0, N // unroll, pass1, 0)

    def pass2(n, carry):
        r0 = pl.multiple_of(n * C, C)
        heads = list(range(hb))
        s_olds = [s_ref[0, hi] for hi in heads]
        rs = [_dot(lhs_scr[hi, n], s_old) for hi, s_old in zip(heads, s_olds)]
        for hi, s_old, r in zip(heads, s_olds, rs):
            glast = gch_ref[0, hi, pl.ds(n, 1), :][:, C - 1:C]
            s_ref[0, hi] = s_old * jnp.exp(glast) - r[:D] + n_scr[hi, n]
        for hi, r in zip(heads, rs):
            cols = slice(hi * D, (hi + 1) * D)
            o = o0_scr[hi, n] + r[D:]
            o_ref[pl.ds(r0, C), cols] = _finish(o, z_ref[pl.ds(r0, C), cols], ong).astype(o_ref.dtype)
        return carry

    lax.fori_loop(0, N, pass2, 0)


def _delta_units_kernel(q_ref, k_ref, v_ref, z_ref, hq_ref, hk_ref, hv_ref, wq_ref, wk_ref, wv_ref,
                        gch_ref, bth_ref, cf_ref, s0_ref, ong_ref, o_ref, s_ref, *, hb, unit):
    C, D = CHUNK, LANE
    n_units = C // unit
    masks = _masks(C, unit)
    n_fac = max(int(math.log2(unit)) - 1, 0)
    ong = ong_ref[...]
    pos = lax.broadcasted_iota(jnp.int32, (C, 1), 0) % unit
    his = list(range(hb))
    colss = [slice(hi * D, (hi + 1) * D) for hi in his]

    def conv(x_ref, h_ref, w_ref):
        xs = [x_ref[:, cols] for cols in colss]
        halos = [h_ref[:, cols] for cols in colss]
        shifteds = [lambda s, x=x, halo=halo: jnp.where(pos >= s, pltpu.roll(x, s, axis=0),
                                                        pltpu.roll(halo, C + s - CONV_HALO, axis=0))
                    for x, halo in zip(xs, halos)]
        return _conv_silu(xs, shifteds, w_ref, colss)

    qs = [_l2norm(y) * (D ** -0.5) for y in conv(q_ref, hq_ref, wq_ref)]
    ks = [_l2norm(y) for y in conv(k_ref, hk_ref, wk_ref)]
    vs = conv(v_ref, hv_ref, wv_ref)
    cf = cf_ref[0]
    wus, q_effs, o0s, k_decs = _block_terms(qs, ks, vs, [cf] * hb, his, [gch_ref[0, hi] for hi in his],
                                            [bth_ref[0, hi] for hi in his], masks, n_fac)
    units = [(hi, u) for u in range(n_units) for hi in his]
    rowss = [slice(u * unit, (u + 1) * unit) for _, u in units]
    s_olds = [s0_ref[u, hi] for hi, u in units]
    xs = [_dot(jnp.concatenate([wus[hi][rows, :D], q_effs[hi][rows]], axis=0), s_old)
          for (hi, _), rows, s_old in zip(units, rowss, s_olds)]
    upds = [_dot_t0(k_decs[hi][rows], wus[hi][rows, D:] - x[:unit]) for (hi, _), rows, x in zip(units, rowss, xs)]
    for (hi, u), s_old, upd in zip(units, s_olds, upds):
        glast = cf[u * unit:u * unit + 1, CF_GLAST + hi:CF_GLAST + hi + 1]
        s_ref[u, hi] = s_old * jnp.exp(glast) + upd
    for hi in his:
        o = jnp.concatenate([o0s[hi][rows] + x[unit:] for (h2, _), rows, x in zip(units, rowss, xs) if h2 == hi],
                            axis=0)
        o_ref[:, colss[hi]] = _finish(o, z_ref[:, colss[hi]], ong).astype(o_ref.dtype)


def _delta_seq(proj, offs, conv_halo, w_conv8, gc_hm, beta_hm, cf, o_norm_g, *, hb, lt, unroll):
    T = proj.shape[0]
    B, H, N, _ = gc_hm.shape
    D = LANE
    W = hb * D
    L = T // B
    n_t = L // lt
    nt = lt // CHUNK
    hpt = lt // CONV_HALO

    def col_spec(off):
        return pl.BlockSpec((lt, W), lambda i, j, t, o=off // W: (i * n_t + t, o + j))

    def prev_spec(off):
        return pl.BlockSpec((CONV_HALO, W),
                            lambda i, j, t, o=off // W: (jnp.maximum((i * n_t + t) * hpt - 1, 0), o + j))

    def halo_spec(g):
        return pl.BlockSpec((1, CONV_HALO, W), lambda i, j, t, o=g * (H // hb): (i, 0, o + j))

    def w_spec(g):
        return pl.BlockSpec((CONV_HALO, W), lambda i, j, t, o=g * (H // hb): (0, o + j))

    hm_spec = pl.BlockSpec((1, hb, nt, CHUNK), lambda i, j, t: (i, j, t, 0))
    state_spec = pl.BlockSpec((1, hb, D, D), lambda i, j, t: (i, j, 0, 0))
    in_specs = ([col_spec(o) for o in offs] + [prev_spec(o) for o in offs[:3]] + [halo_spec(g) for g in range(3)]
                + [w_spec(g) for g in range(3)]
                + [hm_spec, hm_spec, pl.BlockSpec((1, lt, LANE), lambda i, j, t: (j, i * n_t + t, 0)),
                   pl.BlockSpec((1, D), lambda i, j, t: (0, 0))])
    args = [proj] * 7 + [conv_halo] * 3 + [w_conv8] * 3 + [gc_hm, beta_hm, cf, o_norm_g.reshape(1, D)]
    blocks = [4 * _nbytes((lt, W), F32), 9 * _nbytes((CONV_HALO, W), F32),
              2 * _nbytes((hb, max(nt, SUBLANE), LANE), F32), _nbytes((lt, LANE), F32),
              _nbytes((hb, D, D), F32), _nbytes((lt, W), BF16)]
    scratch = [pltpu.VMEM((hb, nt, D + CHUNK, D), F32), pltpu.VMEM((hb, nt, D, D), F32),
               pltpu.VMEM((hb, nt, CHUNK, D), F32)]
    return pl.pallas_call(
        functools.partial(_delta_seq_kernel, hb=hb, L=lt, unroll=unroll),
        grid=(B, H // hb, n_t),
        in_specs=in_specs,
        out_specs=[pl.BlockSpec((lt, W), lambda i, j, t: (i * n_t + t, j)), state_spec],
        out_shape=[jax.ShapeDtypeStruct((T, H * D), BF16), jax.ShapeDtypeStruct((B, H, D, D), F32)],
        scratch_shapes=scratch,
        compiler_params=_params(3, blocks, hb * nt * _nbytes((2 * D + 2 * CHUNK, D), F32)),
        name="delta_seq",
    )(*args)


def _delta_units(proj, offs, conv_halo, w_conv8, gc_hm, beta_hm, cf, s0, o_norm_g, *, unit, hb):
    T = proj.shape[0]
    nb, H, _, _ = gc_hm.shape
    D = LANE
    W = hb * D
    n_units = CHUNK // unit

    def col_spec(off):
        return pl.BlockSpec((CHUNK, W), lambda i, j, o=off // W: (i, o + j))

    def halo_spec(g):
        return pl.BlockSpec((CHUNK, W), lambda i, j, o=g * (H // hb): (i, o + j))

    def w_spec(g):
        return pl.BlockSpec((CONV_HALO, W), lambda i, j, o=g * (H // hb): (0, o + j))

    hm_spec = pl.BlockSpec((1, hb, 1, CHUNK), lambda i, j: (i, j, 0, 0))
    state_spec = pl.BlockSpec((n_units, hb, D, D), lambda i, j: (i, j, 0, 0))
    in_specs = ([col_spec(o) for o in offs] + [halo_spec(g) for g in range(3)] + [w_spec(g) for g in range(3)]
                + [hm_spec, hm_spec, pl.BlockSpec((1, CHUNK, LANE), lambda i, j: (j, i, 0)), state_spec,
                   pl.BlockSpec((1, D), lambda i, j: (0, 0))])
    args = [proj] * 4 + [conv_halo] * 3 + [w_conv8] * 3 + [gc_hm, beta_hm, cf, s0, o_norm_g.reshape(1, D)]
    blocks = [7 * _nbytes((CHUNK, W), F32), 3 * _nbytes((CONV_HALO, W), F32),
              2 * _nbytes((hb, SUBLANE, LANE), F32), _nbytes((CHUNK, LANE), F32),
              2 * _nbytes((n_units, hb, D, D), F32), _nbytes((CHUNK, W), BF16)]
    return pl.pallas_call(
        functools.partial(_delta_units_kernel, hb=hb, unit=unit),
        grid=(nb, H // hb),
        in_specs=in_specs,
        out_specs=[pl.BlockSpec((CHUNK, W), lambda i, j: (i, j)), state_spec],
        out_shape=[jax.ShapeDtypeStruct((T, H * D), BF16),
                   jax.ShapeDtypeStruct((nb * n_units, H, D, D), F32)],
        compiler_params=_params(2, blocks),
        name="delta_units",
    )(*args)


def _head_major(tok, nb, H, lane0):
    T = tok.shape[0]
    x = tok[:, lane0:lane0 + H].reshape(nb, T // nb // CHUNK, CHUNK, H)
    return jnp.transpose(x, (0, 3, 1, 2))


def _col_form(beta_tok, gc_tok, gl_tok, H, hb):
    T = beta_tok.shape[0]

    def part(tok, lane0, width):
        x = tok[:, lane0:lane0 + H].reshape(T, H // hb, hb)
        x = jnp.transpose(x, (1, 0, 2))
        return jnp.pad(x, ((0, 0), (0, 0), (0, width - hb)))

    return jnp.concatenate([part(beta_tok, 0, CF_GC - CF_BETA), part(gc_tok, H, CF_GLAST - CF_GC),
                            part(gl_tok, H, LANE - CF_GLAST)], axis=-1)


def _tail_rows(state, new, n):
    L = new.shape[1]
    if L >= n:
        return new[:, L - n:]
    return jnp.concatenate([state[:, state.shape[1] - (n - L):], new], axis=1)


def _layer(x, p, pool_state, conv_state, delta_state, start, lw, cfg):
    (w_in_b, w_tail_b, w_pool_b, pool_scale, w_conv8, alog_row, dtb_row, o_norm_g, w_out_b, ln1_g, ln1_b,
     w_gate_up_b, w_down_b, ln2_g, ln2_b, w_ple_gate_b, w_ple_proj_b, alpha) = lw
    B, L, D = x.shape
    T = B * L
    P, DN, H, d_ff = cfg["P"], cfg["DN"], cfg["H"], cfg["d_ff"]
    n_main = P + 4 * DN
    tm = _pick(T, (1024, 512, 256, 128))
    tm_ln = min(tm, 256)
    tm_down = min(tm, 512)
    tn_d = _pick(D, (512, 256, 128))
    tn_ff = _pick(d_ff, (256, 128))
    x2 = x.reshape(T, D)
    xb = x2.astype(BF16)

    proj = _matmul(xb, w_in_b, n_main, tm=tm, tn=_pick(n_main, (512, 256, 128)), out_dtype=F32)
    proj3 = proj.reshape(B, L, n_main)
    n_pool, n_conv = pool_state.shape[1], conv_state.shape[1]
    new_pool = _tail_rows(pool_state, proj3[:, :, :P], n_pool)
    new_conv = _tail_rows(conv_state, proj3[:, :, P:P + 3 * DN], n_conv)

    pool_halo = jnp.pad(pool_state, ((0, 0), (POOL_HALO - n_pool, 0), (0, 0)))
    bb_pool = 1 if L >= 256 else _pick(B, (16, 8, 4, 2, 1))
    y_pool = _pool_mixer(proj, pool_halo, w_pool_b, pool_scale, bb=bb_pool, L=L, start=start)

    unit = math.gcd(L, CHUNK)
    beta_tok, gc_tok, gl_tok = _gates(xb, w_tail_b, alog_row, dtb_row, unit=unit, tm=tm)
    conv_halo = jnp.pad(conv_state, ((0, 0), (CONV_HALO - n_conv, 0), (0, 0)))
    offs = (P, P + DN, P + 2 * DN, P + 3 * DN)
    if unit == CHUNK:
        assert delta_state is None
        hb = _pick(H, (8, 4, 2, 1))
        o, s_new = _delta_seq(proj, offs, conv_halo, w_conv8, _head_major(gc_tok, B, H, H),
                              _head_major(beta_tok, B, H, 0), _col_form(beta_tok, gc_tok, gl_tok, H, hb),
                              o_norm_g, hb=hb, lt=_pick(L, (512, 256, 128, 64)), unroll=1)
    else:
        assert unit == CONV_HALO and T % CHUNK == 0
        nb, hb = T // CHUNK, _pick(H, (4, 2, 1))
        o, s_new = _delta_units(proj, offs, conv_halo.reshape(B * CONV_HALO, 3 * DN), w_conv8,
                                _head_major(gc_tok, nb, H, H), _head_major(beta_tok, nb, H, 0),
                                _col_form(beta_tok, gc_tok, gl_tok, H, hb), delta_state, o_norm_g,
                                unit=unit, hb=hb)

    s1 = _matmul_resid((y_pool, o), w_out_b, x2, alpha, tm=tm, tn=tn_d)
    h1, h1b = _layer_norm(s1, ln1_g, ln1_b, tm=tm_ln)
    act = _swiglu(h1b, w_gate_up_b, d_ff, tm=tm, tn=tn_ff)
    s2 = _matmul_resid((act,), w_down_b, h1, alpha, tm=tm_down, tn=tn_ff)
    h2, h2b = _layer_norm(s2, ln2_g, ln2_b, tm=tm_ln)
    y = _ple(h2b, h2, p.reshape(T, -1).astype(BF16), w_ple_gate_b, w_ple_proj_b, tm=tm, tn=tn_d)
    return y.reshape(B, L, D), new_pool, new_conv, s_new


def kernel(x_prompt, x_sample, state_pool, state_conv, state_delta, p_prompt, p_sample, w_in, w_pool, pool_scale, w_conv, a_log, dt_bias, o_norm_g, w_out, ln1_g, ln1_b, w_gate_up, w_down, ln2_g, ln2_b, w_ple_gate, w_ple_proj):
    depth = w_in.shape[0]
    B = x_prompt.shape[0]
    D = x_prompt.shape[2]
    P = w_pool.shape[1] * w_pool.shape[2]
    H = a_log.shape[1]
    DN = H * LANE
    n_main = P + 4 * DN
    cfg = dict(P=P, DN=DN, H=H, d_ff=w_down.shape[1])
    alpha = (2.0 * depth) ** 0.25
    assert w_in.shape[2] == n_main + 2 * H and H <= CF_GC - CF_BETA and n_main % LANE == 0
    assert w_conv.shape[1] == 4 and D == P + DN and max(POOL_WINDOWS) - 1 == state_pool.shape[2]
    assert o_norm_g.shape[1] == LANE

    yp, ys = x_prompt, x_sample
    outs = [[] for _ in range(6)]
    for li in range(depth):
        lw = (w_in[li, :, :n_main].astype(BF16),
              jnp.pad(w_in[li, :, n_main:], ((0, 0), (0, LANE - 2 * H))).astype(BF16),
              w_pool[li].astype(BF16), pool_scale[li],
              jnp.pad(w_conv[li], ((0, CONV_HALO - w_conv.shape[1]), (0, 0))),
              jnp.pad(a_log[li], (H, LANE - 2 * H)).reshape(1, LANE),
              jnp.pad(dt_bias[li], (H, LANE - 2 * H)).reshape(1, LANE),
              o_norm_g[li], w_out[li].astype(BF16), ln1_g[li], ln1_b[li], w_gate_up[li].astype(BF16),
              w_down[li].astype(BF16), ln2_g[li], ln2_b[li], w_ple_gate[li].astype(BF16),
              w_ple_proj[li].astype(BF16), alpha)
        zeros_pool = jnp.zeros((B,) + state_pool.shape[2:], F32)
        zeros_conv = jnp.zeros((B,) + state_conv.shape[2:], F32)
        yp, npl, ncv, nst = _layer(yp, p_prompt[li], zeros_pool, zeros_conv, None, 0, lw, cfg)
        ys, spl, scv, sst = _layer(ys, p_sample[li], state_pool[li], state_conv[li], state_delta[li], PAST_LEN,
                                   lw, cfg)
        for acc, val in zip(outs, (npl, ncv, nst, spl, scv, sst)):
            acc.append(val)
    return (yp, ys) + tuple(jnp.stack(o) for o in outs)
```

```python
import functools
import math

import jax
import jax.numpy as jnp
from jax import lax
from jax.experimental import pallas as pl
from jax.experimental.pallas import tpu as pltpu

F32 = jnp.float32
BF16 = jnp.bfloat16

PAST_LEN = 16384
POOL_WINDOWS = (2, 4, 8, 16)
POOL_HALO = 16
CONV_HALO = 8
CHUNK = 64
LN_EPS = 1e-5
RMS_EPS = 1e-6
L2_EPS = 1e-6

LANE = 128
SUBLANE = 8
VMEM_BUDGET = 56 * 1024 * 1024

CF_BETA, CF_GC, CF_GLAST = 0, 32, 64


def _vmem_limit(block_bytes, scratch_bytes=0):
    est = 2 * sum(block_bytes) + scratch_bytes + (12 << 20)
    return int(min(est, VMEM_BUDGET))


def _nbytes(shape, dtype):
    return math.prod(shape) * jnp.dtype(dtype).itemsize


def _pick(n, candidates):
    for c in candidates:
        if n % c == 0:
            return c
    raise ValueError(f"no tile for {n} among {candidates}")


def _params(n_axes, block_bytes, scratch_bytes=0):
    return pltpu.CompilerParams(dimension_semantics=("arbitrary",) * n_axes,
                                vmem_limit_bytes=_vmem_limit(block_bytes, scratch_bytes))


def _dot(a, b):
    return jnp.dot(a, b, preferred_element_type=F32)


def _dot_t0(a, b):
    return lax.dot_general(a, b, (((0,), (0,)), ((), ())), preferred_element_type=F32)


MM_SUB_ROWS = 1024


def _mm_kernel(x_ref, w_ref, o_ref):
    w = w_ref[...].astype(BF16)
    tm = x_ref.shape[0]
    sub = min(tm, MM_SUB_ROWS)
    for r in range(0, tm, sub):
        o_ref[r:r + sub, :] = _dot(x_ref[r:r + sub, :], w).astype(o_ref.dtype)


def _matmul(x, w, n_cols, *, tm, tn, out_dtype):
    T, K = x.shape
    blocks = [_nbytes((tm, K), x.dtype), _nbytes((K, tn), w.dtype), _nbytes((tm, tn), out_dtype)]
    return pl.pallas_call(
        _mm_kernel,
        grid=(T // tm, n_cols // tn),
        in_specs=[pl.BlockSpec((tm, K), lambda i, j: (i, 0)),
                  pl.BlockSpec((K, tn), lambda i, j: (0, j))],
        out_specs=pl.BlockSpec((tm, tn), lambda i, j: (i, j)),
        out_shape=jax.ShapeDtypeStruct((T, n_cols), out_dtype),
        compiler_params=_params(2, blocks, _nbytes((tm, tn), F32)),
        name="matmul",
    )(x, w)


def _mm_resid_kernel(*refs, alpha, splits):
    x_refs, (w_ref, r_ref, o_ref) = refs[:len(splits)], refs[len(splits):]
    acc = alpha * r_ref[...]
    k0 = 0
    for x_ref, kw in zip(x_refs, splits):
        acc = acc + _dot(x_ref[...], w_ref[k0:k0 + kw, :])
        k0 += kw
    o_ref[...] = acc


def _matmul_resid(xs, w, resid, alpha, *, tm, tn):
    T = xs[0].shape[0]
    splits = tuple(x.shape[1] for x in xs)
    K, N = w.shape
    blocks = [_nbytes((tm, K), BF16), _nbytes((K, tn), BF16), 2 * _nbytes((tm, tn), F32)]
    return pl.pallas_call(
        functools.partial(_mm_resid_kernel, alpha=alpha, splits=splits),
        grid=(T // tm, N // tn),
        in_specs=[pl.BlockSpec((tm, kw), lambda i, j: (i, 0)) for kw in splits]
        + [pl.BlockSpec((K, tn), lambda i, j: (0, j)), pl.BlockSpec((tm, tn), lambda i, j: (i, j))],
        out_specs=pl.BlockSpec((tm, tn), lambda i, j: (i, j)),
        out_shape=jax.ShapeDtypeStruct((T, N), F32),
        compiler_params=_params(2, blocks, _nbytes((tm, tn), F32)),
        name="matmul_resid",
    )(*xs, w, resid)


def _ln_kernel(s_ref, g_ref, b_ref, h_ref, hb_ref):
    s = s_ref[...]
    mu = jnp.mean(s, -1, keepdims=True)
    c = s - mu
    var = jnp.mean(c * c, -1, keepdims=True)
    y = c * lax.rsqrt(var + LN_EPS) * g_ref[...] + b_ref[...]
    h_ref[...] = y
    hb_ref[...] = y.astype(BF16)


def _layer_norm(s, g, b, *, tm):
    T, D = s.shape
    blocks = [_nbytes((tm, D), F32)] * 2 + [_nbytes((tm, D), BF16), 2 * _nbytes((1, D), F32)]
    row = pl.BlockSpec((tm, D), lambda i: (i, 0))
    vec = pl.BlockSpec((1, D), lambda i: (0, 0))
    return pl.pallas_call(
        _ln_kernel,
        grid=(T // tm,),
        in_specs=[row, vec, vec],
        out_specs=[row, row],
        out_shape=[jax.ShapeDtypeStruct((T, D), F32), jax.ShapeDtypeStruct((T, D), BF16)],
        compiler_params=_params(1, blocks, 2 * _nbytes((tm, D), F32)),
        name="layer_norm",
    )(s, g.reshape(1, D), b.reshape(1, D))


def _swiglu_kernel(h_ref, wg_ref, wu_ref, o_ref):
    h = h_ref[...]
    gate = _dot(h, wg_ref[...].astype(BF16))
    up = _dot(h, wu_ref[...].astype(BF16))
    o_ref[...] = (gate * jax.nn.sigmoid(gate) * up).astype(o_ref.dtype)


def _swiglu(h, w_gate_up, d_ff, *, tm, tn):
    T, K = h.shape
    n_j = d_ff // tn
    blocks = [_nbytes((tm, K), BF16), 2 * _nbytes((K, tn), w_gate_up.dtype), _nbytes((tm, tn), BF16)]
    return pl.pallas_call(
        _swiglu_kernel,
        grid=(T // tm, n_j),
        in_specs=[pl.BlockSpec((tm, K), lambda i, j: (i, 0)),
                  pl.BlockSpec((K, tn), lambda i, j: (0, j)),
                  pl.BlockSpec((K, tn), lambda i, j: (0, j + n_j))],
        out_specs=pl.BlockSpec((tm, tn), lambda i, j: (i, j)),
        out_shape=jax.ShapeDtypeStruct((T, d_ff), BF16),
        compiler_params=_params(2, blocks, 3 * _nbytes((tm, tn), F32)),
        name="swiglu",
    )(h, w_gate_up, w_gate_up)


def _ple_kernel(hb_ref, wg_ref, p_ref, wp_ref, h_ref, y_ref):
    gate = jax.nn.sigmoid(_dot(hb_ref[...], wg_ref[...]))
    e = _dot(p_ref[...], wp_ref[...])
    y_ref[...] = h_ref[...] + gate * e


def _ple(hb, h, p, w_gate, w_proj, *, tm, tn):
    T, D = h.shape
    P = p.shape[1]
    blocks = [_nbytes((tm, D), BF16), _nbytes((D, tn), BF16), _nbytes((tm, P), BF16), _nbytes((P, tn), BF16),
              2 * _nbytes((tm, tn), F32)]
    return pl.pallas_call(
        _ple_kernel,
        grid=(T // tm, D // tn),
        in_specs=[pl.BlockSpec((tm, D), lambda i, j: (i, 0)),
                  pl.BlockSpec((D, tn), lambda i, j: (0, j)),
                  pl.BlockSpec((tm, P), lambda i, j: (i, 0)),
                  pl.BlockSpec((P, tn), lambda i, j: (0, j)),
                  pl.BlockSpec((tm, tn), lambda i, j: (i, j))],
        out_specs=pl.BlockSpec((tm, tn), lambda i, j: (i, j)),
        out_shape=jax.ShapeDtypeStruct((T, D), F32),
        compiler_params=_params(2, blocks, 3 * _nbytes((tm, tn), F32)),
        name="ple",
    )(hb, w_gate, p, w_proj, h)


def _pool_kernel(u_ref, halo_ref, w_ref, scale_ref, o_ref, d_ref, *, bb, L, tl, start):
    P = u_ref.shape[1]
    G = len(POOL_WINDOWS)
    gw = P // G
    n_sub = L // tl
    row = lax.broadcasted_iota(jnp.int32, (tl, 1), 0)

    def sub_tile(bi, t, tail):
        r0 = pl.multiple_of(bi * L + t * tl, SUBLANE)
        x = u_ref[pl.ds(r0, tl), :]
        ext = jnp.concatenate([tail, x], axis=0)
        pos = start + t * tl + row
        for gi, win in enumerate(POOL_WINDOWS):
            cols = slice(gi * gw, (gi + 1) * gw)
            s = ext[:, cols]
            shift = 1
            while shift < win:
                s = s + pltpu.roll(s, shift, axis=0)
                shift *= 2
            cnt = jnp.minimum(pos + 1, win).astype(F32)
            d = s[POOL_HALO:] / cnt - x[:, cols]
            d_ref[pl.ds(r0, tl), cols] = d
        return ext[tl:]

    def seq(bi, carry):
        tail0 = halo_ref[bi]
        if n_sub == 1:
            sub_tile(bi, 0, tail0)
        else:
            lax.fori_loop(0, n_sub, lambda t, tail: sub_tile(bi, t, tail), tail0)
        return carry

    lax.fori_loop(0, bb, seq, 0)
    for gi in range(G):
        cols = slice(gi * gw, (gi + 1) * gw)
        y = _dot(d_ref[:, cols].astype(BF16), w_ref[gi])
        o_ref[:, cols] = (y * scale_ref[:, cols]).astype(o_ref.dtype)


def _pool_mixer(proj, halo, w_pool, pool_scale, *, bb, L, start):
    B, _, P = halo.shape
    G, gw, _ = w_pool.shape
    rows = bb * L
    tl = min(L, 256)
    blocks = [_nbytes((rows, P), F32), _nbytes((bb, POOL_HALO, P), F32), _nbytes(w_pool.shape, BF16),
              _nbytes((1, P), F32), _nbytes((rows, P), BF16)]
    return pl.pallas_call(
        functools.partial(_pool_kernel, bb=bb, L=L, tl=tl, start=start),
        grid=(B // bb,),
        in_specs=[pl.BlockSpec((rows, P), lambda i: (i, 0)),
                  pl.BlockSpec((bb, POOL_HALO, P), lambda i: (i, 0, 0)),
                  pl.BlockSpec((G, gw, gw), lambda i: (0, 0, 0)),
                  pl.BlockSpec((1, P), lambda i: (0, 0))],
        out_specs=pl.BlockSpec((rows, P), lambda i: (i, 0)),
        out_shape=jax.ShapeDtypeStruct((B * L, P), BF16),
        scratch_shapes=[pltpu.VMEM((rows, P), F32)],
        compiler_params=_params(1, blocks, _nbytes((rows, P), F32) + 4 * _nbytes((tl + POOL_HALO, P), F32)),
        name="pool_mixer",
    )(proj, halo, w_pool, pool_scale.reshape(1, P))


def _gates_kernel(x_ref, w_ref, alog_ref, dtb_ref, beta_ref, gc_ref, gl_ref, *, unit):
    t = _dot(x_ref[...], w_ref[...])
    beta_ref[...] = jax.nn.sigmoid(t)
    x = t + dtb_ref[...]
    softplus = jnp.maximum(x, 0.0) + jnp.log1p(jnp.exp(-jnp.abs(x)))
    g = -jnp.exp(alog_ref[...]) * softplus
    rows = g.shape[0]
    pos = lax.broadcasted_iota(jnp.int32, g.shape, 0) % unit
    shift = 1
    while shift < unit:
        g = g + jnp.where(pos >= shift, pltpu.roll(g, shift, axis=0), 0.0)
        shift *= 2
    gc_ref[...] = g
    shift = 1
    while shift < unit:
        g = jnp.where((pos & shift) == 0, pltpu.roll(g, rows - shift, axis=0), g)
        shift *= 2
    gl_ref[...] = g


def _gates(x, w_tail, alog_row, dtb_row, *, unit, tm):
    T, D = x.shape
    spec = pl.BlockSpec((tm, LANE), lambda i: (i, 0))
    vec = pl.BlockSpec((1, LANE), lambda i: (0, 0))
    blocks = [_nbytes((tm, D), BF16), _nbytes((D, LANE), BF16), 3 * _nbytes((tm, LANE), F32)]
    return pl.pallas_call(
        functools.partial(_gates_kernel, unit=unit),
        grid=(T // tm,),
        in_specs=[pl.BlockSpec((tm, D), lambda i: (i, 0)), pl.BlockSpec((D, LANE), lambda i: (0, 0)), vec, vec],
        out_specs=[spec, spec, spec],
        out_shape=[jax.ShapeDtypeStruct((T, LANE), F32)] * 3,
        compiler_params=_params(1, blocks, 8 * _nbytes((tm, LANE), F32)),
        name="gates",
    )(x, w_tail, alog_row, dtb_row)


def _zmap(fn, *lists):
    return [fn(*args) for args in zip(*lists)]


def _block_terms(qs, ks, vs, cfs, his, grows, brows, masks, n_fac):
    incl, strict, eye = masks
    C, D = qs[0].shape
    bcols = [cf[:, CF_BETA + hi:CF_BETA + hi + 1] for cf, hi in zip(cfs, his)]
    gcols = [cf[:, CF_GC + hi:CF_GC + hi + 1] for cf, hi in zip(cfs, his)]
    glcols = [cf[:, CF_GLAST + hi:CF_GLAST + hi + 1] for cf, hi in zip(cfs, his)]
    decays = _zmap(lambda gc, gr: jnp.where(incl, jnp.exp(jnp.where(incl, gc - gr, 0.0)), 0.0), gcols, grows)
    qk2s = _zmap(lambda q, k: lax.dot_general(jnp.concatenate([q, k], axis=0), k, (((1,), (1,)), ((), ())),
                                              preferred_element_type=F32), qs, ks)
    a_s = _zmap(lambda qk2, dc, bc: jnp.where(strict, qk2[C:] * dc * bc, 0.0), qk2s, decays, bcols)
    ps = [jnp.where(eye, 1.0, 0.0) - a for a in a_s]
    if n_fac > 0:
        sqs = [_dot(a, a) for a in a_s]
        for f in range(n_fac):
            if f < n_fac - 1:
                rs = _zmap(lambda p, sq: _dot(jnp.concatenate([p, sq], axis=0), sq), ps, sqs)
                ps = _zmap(lambda p, r: p + r[:C], ps, rs)
                sqs = [r[C:] for r in rs]
            else:
                ps = _zmap(lambda p, sq: p + _dot(p, sq), ps, sqs)
    tbs = _zmap(lambda p, br: p * br, ps, brows)
    w1s = _zmap(lambda tb, gr, k: _dot(tb * jnp.exp(gr), k), tbs, grows, ks)
    u1s = _zmap(_dot, tbs, vs)
    wus = _zmap(lambda w1, u1: jnp.concatenate([w1, u1], axis=1), w1s, u1s)
    tops = _zmap(lambda qk2, dc, wu: _dot(jnp.where(incl, qk2[:C] * dc, 0.0), wu), qk2s, decays, wus)
    q_effs = _zmap(lambda q, gc, top: q * jnp.exp(gc) - top[:, :D], qs, gcols, tops)
    k_decs = _zmap(lambda k, gl, gc: k * jnp.exp(gl - gc), ks, glcols, gcols)
    return wus, q_effs, [top[:, D:] for top in tops], k_decs


def _conv_silu(xs, shifteds, w_ref, colss):
    ys = _zmap(lambda x, cols: x * w_ref[3:4, cols], xs, colss)
    for s in (1, 2, 3):
        ys = _zmap(lambda y, sh, cols: y + sh(s) * w_ref[3 - s:4 - s, cols], ys, shifteds, colss)
    return [y * jax.nn.sigmoid(y) for y in ys]


def _l2norm(x):
    return x * lax.rsqrt(jnp.sum(x * x, -1, keepdims=True) + L2_EPS)


def _finish(o, z, ong):
    o = o * lax.rsqrt(jnp.mean(o * o, -1, keepdims=True) + RMS_EPS) * ong
    return o * (z * jax.nn.sigmoid(z))


def _masks(C, unit):
    ri = lax.broadcasted_iota(jnp.int32, (C, C), 0)
    ci = lax.broadcasted_iota(jnp.int32, (C, C), 1)
    same = (ri // unit) == (ci // unit)
    return same & (ri >= ci), same & (ri > ci), ri == ci


def _delta_seq_kernel(q_ref, k_ref, v_ref, z_ref, pq_ref, pk_ref, pv_ref, hq_ref, hk_ref, hv_ref,
                      wq_ref, wk_ref, wv_ref, gch_ref, bth_ref, cf_ref, ong_ref, o_ref, s_ref,
                      lhs_scr, n_scr, o0_scr, *, hb, L, unroll):
    C, D = CHUNK, LANE
    N = L // C
    masks = _masks(C, C)
    n_fac = int(math.log2(C)) - 1
    ong = ong_ref[...]
    first_tile = pl.program_id(2) == 0

    @pl.when(first_tile)
    def _():
        for hi in range(hb):
            s_ref[0, hi] = jnp.zeros((D, D), F32)

    def pass1(g, carry):
        items = [(g * unroll + u, hi) for u in range(unroll) for hi in range(hb)]
        ns = [n for n, _ in items]
        his = [hi for _, hi in items]
        colss = [slice(hi * D, (hi + 1) * D) for hi in his]
        r0s = [pl.multiple_of(n * C, C) for n in ns]
        rps = [pl.multiple_of(jnp.maximum(r0 - CONV_HALO, 0), SUBLANE) for r0 in r0s]

        def conv(x_ref, p_ref, h_ref, w_ref):
            xs = _zmap(lambda r0, cols: x_ref[pl.ds(r0, C), cols], r0s, colss)
            prevs = _zmap(lambda n, rp, cols: jnp.where(
                n > 0, x_ref[pl.ds(rp, CONV_HALO), cols],
                jnp.where(first_tile, h_ref[0, :, cols], p_ref[:, cols])), ns, rps, colss)
            exts = _zmap(lambda prev, x: jnp.concatenate([prev, x], axis=0), prevs, xs)
            shifteds = [lambda s, ext=ext: pltpu.roll(ext, s, axis=0)[CONV_HALO:] for ext in exts]
            return _conv_silu(xs, shifteds, w_ref, colss)

        qs = [_l2norm(y) * (D ** -0.5) for y in conv(q_ref, pq_ref, hq_ref, wq_ref)]
        ks = [_l2norm(y) for y in conv(k_ref, pk_ref, hk_ref, wk_ref)]
        vs = conv(v_ref, pv_ref, hv_ref, wv_ref)
        cfs = [cf_ref[0, pl.ds(r0, C), :] for r0 in r0s]
        grows = [gch_ref[0, hi, pl.ds(n, 1), :] for n, hi in items]
        brows = [bth_ref[0, hi, pl.ds(n, 1), :] for n, hi in items]
        wus, q_effs, o0s, k_decs = _block_terms(qs, ks, vs, cfs, his, grows, brows, masks, n_fac)
        mns = _zmap(_dot_t0, k_decs, wus)
        for (n, hi), mn, q_eff, o0 in zip(items, mns, q_effs, o0s):
            lhs_scr[hi, n, :D, :] = mn[:, :D]
            lhs_scr[hi, n, D:, :] = q_eff
            n_scr[hi, n] = mn[:, D:]
            o0_scr[hi, n] = o0
        return carry

    if N == unroll:
        pass1(0, 0)
    else:
        lax.fori_loop(0, N // unroll, pass1, 0)

    def pass2(n, carry):
        r0 = pl.multiple_of(n * C, C)
        heads = list(range(hb))
        s_olds = [s_ref[0, hi] for hi in heads]
        rs = [_dot(lhs_scr[hi, n], s_old) for hi, s_old in zip(heads, s_olds)]
        for hi, s_old, r in zip(heads, s_olds, rs):
            glast = gch_ref[0, hi, pl.ds(n, 1), :][:, C - 1:C]
            s_ref[0, hi] = s_old * jnp.exp(glast) - r[:D] + n_scr[hi, n]
        for hi, r in zip(heads, rs):
            cols = slice(hi * D, (hi + 1) * D)
            o = o0_scr[hi, n] + r[D:]
            o_ref[pl.ds(r0, C), cols] = _finish(o, z_ref[pl.ds(r0, C), cols], ong).astype(o_ref.dtype)
        return carry

    lax.fori_loop(0, N, pass2, 0)


def _delta_units_kernel(q_ref, k_ref, v_ref, z_ref, hq_ref, hk_ref, hv_ref, wq_ref, wk_ref, wv_ref,
                        gch_ref, bth_ref, cf_ref, s0_ref, ong_ref, o_ref, s_ref, *, hb, unit):
    C, D = CHUNK, LANE
    n_units = C // unit
    masks = _masks(C, unit)
    n_fac = max(int(math.log2(unit)) - 1, 0)
    ong = ong_ref[...]
    pos = lax.broadcasted_iota(jnp.int32, (C, 1), 0) % unit
    his = list(range(hb))
    colss = [slice(hi * D, (hi + 1) * D) for hi in his]

    def conv(x_ref, h_ref, w_ref):
        xs = [x_ref[:, cols] for cols in colss]
        halos = [h_ref[:, cols] for cols in colss]
        shifteds = [lambda s, x=x, halo=halo: jnp.where(pos >= s, pltpu.roll(x, s, axis=0),
                                                        pltpu.roll(halo, C + s - CONV_HALO, axis=0))
                    for x, halo in zip(xs, halos)]
        return _conv_silu(xs, shifteds, w_ref, colss)

    qs = [_l2norm(y) * (D ** -0.5) for y in conv(q_ref, hq_ref, wq_ref)]
    ks = [_l2norm(y) for y in conv(k_ref, hk_ref, wk_ref)]
    vs = conv(v_ref, hv_ref, wv_ref)
    cf = cf_ref[0]
    wus, q_effs, o0s, k_decs = _block_terms(qs, ks, vs, [cf] * hb, his, [gch_ref[0, hi] for hi in his],
                                            [bth_ref[0, hi] for hi in his], masks, n_fac)
    units = [(hi, u) for u in range(n_units) for hi in his]
    rowss = [slice(u * unit, (u + 1) * unit) for _, u in units]
    s_olds = [s0_ref[u, hi] for hi, u in units]
    xs = [_dot(jnp.concatenate([wus[hi][rows, :D], q_effs[hi][rows]], axis=0), s_old)
          for (hi, _), rows, s_old in zip(units, rowss, s_olds)]
    upds = [_dot_t0(k_decs[hi][rows], wus[hi][rows, D:] - x[:unit]) for (hi, _), rows, x in zip(units, rowss, xs)]
    for (hi, u), s_old, upd in zip(units, s_olds, upds):
        glast = cf[u * unit:u * unit + 1, CF_GLAST + hi:CF_GLAST + hi + 1]
        s_ref[u, hi] = s_old * jnp.exp(glast) + upd
    for hi in his:
        o = jnp.concatenate([o0s[hi][rows] + x[unit:] for (h2, _), rows, x in zip(units, rowss, xs) if h2 == hi],
                            axis=0)
        o_ref[:, colss[hi]] = _finish(o, z_ref[:, colss[hi]], ong).astype(o_ref.dtype)


def _delta_seq(proj, offs, conv_halo, w_conv8, gc_hm, beta_hm, cf, o_norm_g, *, hb, lt, unroll):
    T = proj.shape[0]
    B, H, N, _ = gc_hm.shape
    D = LANE
    W = hb * D
    L = T // B
    n_t = L // lt
    nt = lt // CHUNK
    hpt = lt // CONV_HALO

    def col_spec(off):
        return pl.BlockSpec((lt, W), lambda i, j, t, o=off // W: (i * n_t + t, o + j))

    def prev_spec(off):
        return pl.BlockSpec((CONV_HALO, W),
                            lambda i, j, t, o=off // W: (jnp.maximum((i * n_t + t) * hpt - 1, 0), o + j))

    def halo_spec(g):
        return pl.BlockSpec((1, CONV_HALO, W), lambda i, j, t, o=g * (H // hb): (i, 0, o + j))

    def w_spec(g):
        return pl.BlockSpec((CONV_HALO, W), lambda i, j, t, o=g * (H // hb): (0, o + j))

    hm_spec = pl.BlockSpec((1, hb, nt, CHUNK), lambda i, j, t: (i, j, t, 0))
    state_spec = pl.BlockSpec((1, hb, D, D), lambda i, j, t: (i, j, 0, 0))
    in_specs = ([col_spec(o) for o in offs] + [prev_spec(o) for o in offs[:3]] + [halo_spec(g) for g in range(3)]
                + [w_spec(g) for g in range(3)]
                + [hm_spec, hm_spec, pl.BlockSpec((1, lt, LANE), lambda i, j, t: (j, i * n_t + t, 0)),
                   pl.BlockSpec((1, D), lambda i, j, t: (0, 0))])
    args = [proj] * 7 + [conv_halo] * 3 + [w_conv8] * 3 + [gc_hm, beta_hm, cf, o_norm_g.reshape(1, D)]
    blocks = [4 * _nbytes((lt, W), F32), 9 * _nbytes((CONV_HALO, W), F32),
              2 * _nbytes((hb, max(nt, SUBLANE), LANE), F32), _nbytes((lt, LANE), F32),
              _nbytes((hb, D, D), F32), _nbytes((lt, W), BF16)]
    scratch = [pltpu.VMEM((hb, nt, D + CHUNK, D), F32), pltpu.VMEM((hb, nt, D, D), F32),
               pltpu.VMEM((hb, nt, CHUNK, D), F32)]
    return pl.pallas_call(
        functools.partial(_delta_seq_kernel, hb=hb, L=lt, unroll=unroll),
        grid=(B, H // hb, n_t),
        in_specs=in_specs,
        out_specs=[pl.BlockSpec((lt, W), lambda i, j, t: (i * n_t + t, j)), state_spec],
        out_shape=[jax.ShapeDtypeStruct((T, H * D), BF16), jax.ShapeDtypeStruct((B, H, D, D), F32)],
        scratch_shapes=scratch,
        compiler_params=_params(3, blocks, hb * nt * _nbytes((2 * D + 2 * CHUNK, D), F32)),
        name="delta_seq",
    )(*args)


def _delta_units(proj, offs, conv_halo, w_conv8, gc_hm, beta_hm, cf, s0, o_norm_g, *, unit, hb):
    T = proj.shape[0]
    nb, H, _, _ = gc_hm.shape
    D = LANE
    W = hb * D
    n_units = CHUNK // unit

    def col_spec(off):
        return pl.BlockSpec((CHUNK, W), lambda i, j, o=off // W: (i, o + j))

    def halo_spec(g):
        return pl.BlockSpec((CHUNK, W), lambda i, j, o=g * (H // hb): (i, o + j))

    def w_spec(g):
        return pl.BlockSpec((CONV_HALO, W), lambda i, j, o=g * (H // hb): (0, o + j))

    hm_spec = pl.BlockSpec((1, hb, 1, CHUNK), lambda i, j: (i, j, 0, 0))
    state_spec = pl.BlockSpec((n_units, hb, D, D), lambda i, j: (i, j, 0, 0))
    in_specs = ([col_spec(o) for o in offs] + [halo_spec(g) for g in range(3)] + [w_spec(g) for g in range(3)]
                + [hm_spec, hm_spec, pl.BlockSpec((1, CHUNK, LANE), lambda i, j: (j, i, 0)), state_spec,
                   pl.BlockSpec((1, D), lambda i, j: (0, 0))])
    args = [proj] * 4 + [conv_halo] * 3 + [w_conv8] * 3 + [gc_hm, beta_hm, cf, s0, o_norm_g.reshape(1, D)]
    blocks = [7 * _nbytes((CHUNK, W), F32), 3 * _nbytes((CONV_HALO, W), F32),
              2 * _nbytes((hb, SUBLANE, LANE), F32), _nbytes((CHUNK, LANE), F32),
              2 * _nbytes((n_units, hb, D, D), F32), _nbytes((CHUNK, W), BF16)]
    return pl.pallas_call(
        functools.partial(_delta_units_kernel, hb=hb, unit=unit),
        grid=(nb, H // hb),
        in_specs=in_specs,
        out_specs=[pl.BlockSpec((CHUNK, W), lambda i, j: (i, j)), state_spec],
        out_shape=[jax.ShapeDtypeStruct((T, H * D), BF16),
                   jax.ShapeDtypeStruct((nb * n_units, H, D, D), F32)],
        compiler_params=_params(2, blocks),
        name="delta_units",
    )(*args)


def _head_major(tok, nb, H, lane0):
    T = tok.shape[0]
    x = tok[:, lane0:lane0 + H].reshape(nb, T // nb // CHUNK, CHUNK, H)
    return jnp.transpose(x, (0, 3, 1, 2))


def _col_form(beta_tok, gc_tok, gl_tok, H, hb):
    T = beta_tok.shape[0]

    def part(tok, lane0, width):
        x = tok[:, lane0:lane0 + H].reshape(T, H // hb, hb)
        x = jnp.transpose(x, (1, 0, 2))
        return jnp.pad(x, ((0, 0), (0, 0), (0, width - hb)))

    return jnp.concatenate([part(beta_tok, 0, CF_GC - CF_BETA), part(gc_tok, H, CF_GLAST - CF_GC),
                            part(gl_tok, H, LANE - CF_GLAST)], axis=-1)


def _tail_rows(state, new, n):
    L = new.shape[1]
    if L >= n:
        return new[:, L - n:]
    return jnp.concatenate([state[:, state.shape[1] - (n - L):], new], axis=1)


def _layer(x, p, pool_state, conv_state, delta_state, start, lw, cfg):
    (w_in, w_tail_b, w_pool_b, pool_scale, w_conv8, alog_row, dtb_row, o_norm_g, w_out_b, ln1_g, ln1_b,
     w_gate_up, w_down_b, ln2_g, ln2_b, w_ple_gate_b, w_ple_proj_b, alpha) = lw
    B, L, D = x.shape
    T = B * L
    P, DN, H, d_ff = cfg["P"], cfg["DN"], cfg["H"], cfg["d_ff"]
    n_main = P + 4 * DN
    tm = _pick(T, (1024, 512, 256, 128))
    tm_ln = min(tm, 256)
    tm_down = min(tm, 512)
    tn_d = _pick(D, (512, 256, 128))
    tn_ff = _pick(d_ff, (256, 128))
    x2 = x.reshape(T, D)
    xb = x2.astype(BF16)

    proj = _matmul(xb, w_in, n_main, tm=_pick(T, (2048, 1024, 512, 256, 128)), tn=_pick(n_main, (256, 128)),
                   out_dtype=F32)
    proj3 = proj.reshape(B, L, n_main)
    n_pool, n_conv = pool_state.shape[1], conv_state.shape[1]
    new_pool = _tail_rows(pool_state, proj3[:, :, :P], n_pool)
    new_conv = _tail_rows(conv_state, proj3[:, :, P:P + 3 * DN], n_conv)

    pool_halo = jnp.pad(pool_state, ((0, 0), (POOL_HALO - n_pool, 0), (0, 0)))
    bb_pool = 1 if L >= 256 else _pick(B, (16, 8, 4, 2, 1))
    y_pool = _pool_mixer(proj, pool_halo, w_pool_b, pool_scale, bb=bb_pool, L=L, start=start)

    unit = math.gcd(L, CHUNK)
    beta_tok, gc_tok, gl_tok = _gates(xb, w_tail_b, alog_row, dtb_row, unit=unit, tm=tm)
    conv_halo = jnp.pad(conv_state, ((0, 0), (CONV_HALO - n_conv, 0), (0, 0)))
    offs = (P, P + DN, P + 2 * DN, P + 3 * DN)
    if unit == CHUNK:
        assert delta_state is None
        hb = _pick(H, (8, 4, 2, 1))
        o, s_new = _delta_seq(proj, offs, conv_halo, w_conv8, _head_major(gc_tok, B, H, H),
                              _head_major(beta_tok, B, H, 0), _col_form(beta_tok, gc_tok, gl_tok, H, hb),
                              o_norm_g, hb=hb, lt=_pick(L, (512, 256, 128, 64)), unroll=2)
    else:
        assert unit == CONV_HALO and T % CHUNK == 0
        nb, hb = T // CHUNK, _pick(H, (4, 2, 1))
        o, s_new = _delta_units(proj, offs, conv_halo.reshape(B * CONV_HALO, 3 * DN), w_conv8,
                                _head_major(gc_tok, nb, H, H), _head_major(beta_tok, nb, H, 0),
                                _col_form(beta_tok, gc_tok, gl_tok, H, hb), delta_state, o_norm_g,
                                unit=unit, hb=hb)

    s1 = _matmul_resid((y_pool, o), w_out_b, x2, alpha, tm=tm, tn=tn_d)
    h1, h1b = _layer_norm(s1, ln1_g, ln1_b, tm=tm_ln)
    act = _swiglu(h1b, w_gate_up, d_ff, tm=tm, tn=tn_ff)
    s2 = _matmul_resid((act,), w_down_b, h1, alpha, tm=tm_down, tn=tn_ff)
    h2, h2b = _layer_norm(s2, ln2_g, ln2_b, tm=tm_ln)
    y = _ple(h2b, h2, p.reshape(T, -1).astype(BF16), w_ple_gate_b, w_ple_proj_b, tm=tm, tn=tn_d)
    return y.reshape(B, L, D), new_pool, new_conv, s_new


def kernel(x_prompt, x_sample, state_pool, state_conv, state_delta, p_prompt, p_sample, w_in, w_pool, pool_scale, w_conv, a_log, dt_bias, o_norm_g, w_out, ln1_g, ln1_b, w_gate_up, w_down, ln2_g, ln2_b, w_ple_gate, w_ple_proj):
    depth = w_in.shape[0]
    B = x_prompt.shape[0]
    D = x_prompt.shape[2]
    P = w_pool.shape[1] * w_pool.shape[2]
    H = a_log.shape[1]
    DN = H * LANE
    n_main = P + 4 * DN
    cfg = dict(P=P, DN=DN, H=H, d_ff=w_down.shape[1])
    alpha = (2.0 * depth) ** 0.25
    assert w_in.shape[2] == n_main + 2 * H and H <= CF_GC - CF_BETA and n_main % LANE == 0
    assert w_conv.shape[1] == 4 and D == P + DN and max(POOL_WINDOWS) - 1 == state_pool.shape[2]
    assert o_norm_g.shape[1] == LANE

    yp, ys = x_prompt, x_sample
    outs = [[] for _ in range(6)]
    for li in range(depth):
        lw = (w_in[li],
              jnp.pad(w_in[li, :, n_main:], ((0, 0), (0, LANE - 2 * H))).astype(BF16),
              w_pool[li].astype(BF16), pool_scale[li],
              jnp.pad(w_conv[li], ((0, CONV_HALO - w_conv.shape[1]), (0, 0))),
              jnp.pad(a_log[li], (H, LANE - 2 * H)).reshape(1, LANE),
              jnp.pad(dt_bias[li], (H, LANE - 2 * H)).reshape(1, LANE),
              o_norm_g[li], w_out[li].astype(BF16), ln1_g[li], ln1_b[li], w_gate_up[li],
              w_down[li].astype(BF16), ln2_g[li], ln2_b[li], w_ple_gate[li].astype(BF16),
              w_ple_proj[li].astype(BF16), alpha)
        zeros_pool = jnp.zeros((B,) + state_pool.shape[2:], F32)
        zeros_conv = jnp.zeros((B,) + state_conv.shape[2:], F32)
        yp, npl, ncv, nst = _layer(yp, p_prompt[li], zeros_pool, zeros_conv, None, 0, lw, cfg)
        ys, spl, scv, sst = _layer(ys, p_sample[li], state_pool[li], state_conv[li], state_delta[li], PAST_LEN,
                                   lw, cfg)
        for acc, val in zip(outs, (npl, ncv, nst, spl, scv, sst)):
            acc.append(val)
    return (yp, ys) + tuple(jnp.stack(o) for o in outs)
```

```python
import functools
import math

import jax
import jax.numpy as jnp
from jax import lax
from jax.experimental import pallas as pl
from jax.experimental.pallas import tpu as pltpu

F32 = jnp.float32
BF16 = jnp.bfloat16

PAST_LEN = 16384
POOL_WINDOWS = (2, 4, 8, 16)
POOL_HALO = 16
CONV_HALO = 8
CHUNK = 64
LN_EPS = 1e-5
RMS_EPS = 1e-6
L2_EPS = 1e-6

LANE = 128
SUBLANE = 8
VMEM_BUDGET = 56 * 1024 * 1024

CF_BETA, CF_GC, CF_GLAST = 0, 32, 64


def _vmem_limit(block_bytes, scratch_bytes=0):
    est = 2 * sum(block_bytes) + scratch_bytes + (12 << 20)
    return int(min(est, VMEM_BUDGET))


def _nbytes(shape, dtype):
    return math.prod(shape) * jnp.dtype(dtype).itemsize


def _pick(n, candidates):
    for c in candidates:
        if n % c == 0:
            return c
    raise ValueError(f"no tile for {n} among {candidates}")


def _params(n_axes, block_bytes, scratch_bytes=0):
    return pltpu.CompilerParams(dimension_semantics=("arbitrary",) * n_axes,
                                vmem_limit_bytes=_vmem_limit(block_bytes, scratch_bytes))


def _dot(a, b):
    return jnp.dot(a, b, preferred_element_type=F32)


def _dot_t0(a, b):
    return lax.dot_general(a, b, (((0,), (0,)), ((), ())), preferred_element_type=F32)


MM_SUB_ROWS = 1024


def _mm_nt_kernel(x_ref, wt_ref, o_ref):
    wt = wt_ref[...].astype(BF16)
    tm = x_ref.shape[0]
    sub = min(tm, MM_SUB_ROWS)
    for r in range(0, tm, sub):
        o_ref[r:r + sub, :] = lax.dot_general(x_ref[r:r + sub, :], wt, (((1,), (1,)), ((), ())),
                                              preferred_element_type=F32).astype(o_ref.dtype)


def _matmul_nt(x, wt, n_cols, *, tm, tn, out_dtype):
    T, K = x.shape
    blocks = [_nbytes((tm, K), x.dtype), _nbytes((tn, K), wt.dtype), _nbytes((tm, tn), out_dtype)]
    return pl.pallas_call(
        _mm_nt_kernel,
        grid=(T // tm, n_cols // tn),
        in_specs=[pl.BlockSpec((tm, K), lambda i, j: (i, 0)),
                  pl.BlockSpec((tn, K), lambda i, j: (j, 0))],
        out_specs=pl.BlockSpec((tm, tn), lambda i, j: (i, j)),
        out_shape=jax.ShapeDtypeStruct((T, n_cols), out_dtype),
        compiler_params=_params(2, blocks, _nbytes((tm, tn), F32) + _nbytes((tn, K), BF16)),
        name="matmul_nt",
    )(x, wt)


def _mm_resid_kernel(*refs, alpha, splits):
    x_refs, (w_ref, r_ref, o_ref) = refs[:len(splits)], refs[len(splits):]
    w = w_ref[...].astype(BF16)
    tm = o_ref.shape[0]
    sub = min(tm, MM_SUB_ROWS)
    for r in range(0, tm, sub):
        acc = alpha * r_ref[r:r + sub, :]
        k0 = 0
        for x_ref, kw in zip(x_refs, splits):
            acc = acc + _dot(x_ref[r:r + sub, :], w[k0:k0 + kw, :])
            k0 += kw
        o_ref[r:r + sub, :] = acc


def _matmul_resid(xs, w, resid, alpha, *, tm, tn):
    T = xs[0].shape[0]
    splits = tuple(x.shape[1] for x in xs)
    K, N = w.shape
    blocks = [_nbytes((tm, K), BF16), _nbytes((K, tn), w.dtype), 2 * _nbytes((tm, tn), F32)]
    return pl.pallas_call(
        functools.partial(_mm_resid_kernel, alpha=alpha, splits=splits),
        grid=(T // tm, N // tn),
        in_specs=[pl.BlockSpec((tm, kw), lambda i, j: (i, 0)) for kw in splits]
        + [pl.BlockSpec((K, tn), lambda i, j: (0, j)), pl.BlockSpec((tm, tn), lambda i, j: (i, j))],
        out_specs=pl.BlockSpec((tm, tn), lambda i, j: (i, j)),
        out_shape=jax.ShapeDtypeStruct((T, N), F32),
        compiler_params=_params(2, blocks, _nbytes((tm, tn), F32)),
        name="matmul_resid",
    )(*xs, w, resid)


def _ln_kernel(s_ref, g_ref, b_ref, h_ref, hb_ref):
    s = s_ref[...]
    mu = jnp.mean(s, -1, keepdims=True)
    c = s - mu
    var = jnp.mean(c * c, -1, keepdims=True)
    y = c * lax.rsqrt(var + LN_EPS) * g_ref[...] + b_ref[...]
    h_ref[...] = y
    hb_ref[...] = y.astype(BF16)


def _layer_norm(s, g, b, *, tm):
    T, D = s.shape
    blocks = [_nbytes((tm, D), F32)] * 2 + [_nbytes((tm, D), BF16), 2 * _nbytes((1, D), F32)]
    row = pl.BlockSpec((tm, D), lambda i: (i, 0))
    vec = pl.BlockSpec((1, D), lambda i: (0, 0))
    return pl.pallas_call(
        _ln_kernel,
        grid=(T // tm,),
        in_specs=[row, vec, vec],
        out_specs=[row, row],
        out_shape=[jax.ShapeDtypeStruct((T, D), F32), jax.ShapeDtypeStruct((T, D), BF16)],
        compiler_params=_params(1, blocks, 2 * _nbytes((tm, D), F32)),
        name="layer_norm",
    )(s, g.reshape(1, D), b.reshape(1, D))


def _swiglu_kernel(h_ref, wg_ref, wu_ref, o_ref):
    h = h_ref[...]
    gate = _dot(h, wg_ref[...].astype(BF16))
    up = _dot(h, wu_ref[...].astype(BF16))
    o_ref[...] = (gate * jax.nn.sigmoid(gate) * up).astype(o_ref.dtype)


def _swiglu(h, w_gate_up, d_ff, *, tm, tn):
    T, K = h.shape
    n_j = d_ff // tn
    blocks = [_nbytes((tm, K), BF16), 2 * _nbytes((K, tn), w_gate_up.dtype), _nbytes((tm, tn), BF16)]
    return pl.pallas_call(
        _swiglu_kernel,
        grid=(T // tm, n_j),
        in_specs=[pl.BlockSpec((tm, K), lambda i, j: (i, 0)),
                  pl.BlockSpec((K, tn), lambda i, j: (0, j)),
                  pl.BlockSpec((K, tn), lambda i, j: (0, j + n_j))],
        out_specs=pl.BlockSpec((tm, tn), lambda i, j: (i, j)),
        out_shape=jax.ShapeDtypeStruct((T, d_ff), BF16),
        compiler_params=_params(2, blocks, 3 * _nbytes((tm, tn), F32)),
        name="swiglu",
    )(h, w_gate_up, w_gate_up)


def _ple_kernel(hb_ref, wg_ref, p_ref, wp_ref, h_ref, y_ref):
    wg = wg_ref[...].astype(BF16)
    wp = wp_ref[...]
    tm = y_ref.shape[0]
    sub = min(tm, MM_SUB_ROWS)
    for r in range(0, tm, sub):
        rows = slice(r, r + sub)
        gate = jax.nn.sigmoid(_dot(hb_ref[rows, :], wg))
        y_ref[rows, :] = h_ref[rows, :] + gate * _dot(p_ref[rows, :], wp)


def _ple(hb, h, p, w_gate, w_proj, *, tm, tn):
    T, D = h.shape
    P = p.shape[1]
    blocks = [_nbytes((tm, D), BF16), _nbytes((D, tn), w_gate.dtype), _nbytes((tm, P), BF16),
              _nbytes((P, tn), BF16), 2 * _nbytes((tm, tn), F32)]
    return pl.pallas_call(
        _ple_kernel,
        grid=(T // tm, D // tn),
        in_specs=[pl.BlockSpec((tm, D), lambda i, j: (i, 0)),
                  pl.BlockSpec((D, tn), lambda i, j: (0, j)),
                  pl.BlockSpec((tm, P), lambda i, j: (i, 0)),
                  pl.BlockSpec((P, tn), lambda i, j: (0, j)),
                  pl.BlockSpec((tm, tn), lambda i, j: (i, j))],
        out_specs=pl.BlockSpec((tm, tn), lambda i, j: (i, j)),
        out_shape=jax.ShapeDtypeStruct((T, D), F32),
        compiler_params=_params(2, blocks, 3 * _nbytes((tm, tn), F32)),
        name="ple",
    )(hb, w_gate, p, w_proj, h)


def _pool_kernel(u_ref, halo_ref, w_ref, scale_ref, o_ref, d_ref, *, bb, L, tl, start):
    P = u_ref.shape[1]
    G = len(POOL_WINDOWS)
    gw = P // G
    n_sub = L // tl
    row = lax.broadcasted_iota(jnp.int32, (tl, 1), 0)

    def sub_tile(bi, t, tail):
        r0 = pl.multiple_of(bi * L + t * tl, SUBLANE)
        x = u_ref[pl.ds(r0, tl), :]
        ext = jnp.concatenate([tail, x], axis=0)
        pos = start + t * tl + row
        for gi, win in enumerate(POOL_WINDOWS):
            cols = slice(gi * gw, (gi + 1) * gw)
            s = ext[:, cols]
            shift = 1
            while shift < win:
                s = s + pltpu.roll(s, shift, axis=0)
                shift *= 2
            cnt = jnp.minimum(pos + 1, win).astype(F32)
            d = s[POOL_HALO:] / cnt - x[:, cols]
            d_ref[pl.ds(r0, tl), cols] = d
        return ext[tl:]

    def seq(bi, carry):
        tail0 = halo_ref[bi]
        if n_sub == 1:
            sub_tile(bi, 0, tail0)
        else:
            lax.fori_loop(0, n_sub, lambda t, tail: sub_tile(bi, t, tail), tail0)
        return carry

    lax.fori_loop(0, bb, seq, 0)
    for gi in range(G):
        cols = slice(gi * gw, (gi + 1) * gw)
        y = _dot(d_ref[:, cols].astype(BF16), w_ref[gi])
        o_ref[:, cols] = (y * scale_ref[:, cols]).astype(o_ref.dtype)


def _pool_mixer(proj, halo, w_pool, pool_scale, *, bb, L, start):
    B, _, P = halo.shape
    G, gw, _ = w_pool.shape
    rows = bb * L
    tl = min(L, 256)
    blocks = [_nbytes((rows, P), F32), _nbytes((bb, POOL_HALO, P), F32), _nbytes(w_pool.shape, BF16),
              _nbytes((1, P), F32), _nbytes((rows, P), BF16)]
    return pl.pallas_call(
        functools.partial(_pool_kernel, bb=bb, L=L, tl=tl, start=start),
        grid=(B // bb,),
        in_specs=[pl.BlockSpec((rows, P), lambda i: (i, 0)),
                  pl.BlockSpec((bb, POOL_HALO, P), lambda i: (i, 0, 0)),
                  pl.BlockSpec((G, gw, gw), lambda i: (0, 0, 0)),
                  pl.BlockSpec((1, P), lambda i: (0, 0))],
        out_specs=pl.BlockSpec((rows, P), lambda i: (i, 0)),
        out_shape=jax.ShapeDtypeStruct((B * L, P), BF16),
        scratch_shapes=[pltpu.VMEM((rows, P), F32)],
        compiler_params=_params(1, blocks, _nbytes((rows, P), F32) + 4 * _nbytes((tl + POOL_HALO, P), F32)),
        name="pool_mixer",
    )(proj, halo, w_pool, pool_scale.reshape(1, P))


def _gates_kernel(x_ref, w_ref, alog_ref, dtb_ref, beta_ref, gc_ref, gl_ref, *, unit):
    t = lax.dot_general(x_ref[...], w_ref[...], (((1,), (1,)), ((), ())),
                        preferred_element_type=F32)
    beta_ref[...] = jax.nn.sigmoid(t)
    x = t + dtb_ref[...]
    softplus = jnp.maximum(x, 0.0) + jnp.log1p(jnp.exp(-jnp.abs(x)))
    g = -jnp.exp(alog_ref[...]) * softplus
    rows = g.shape[0]
    pos = lax.broadcasted_iota(jnp.int32, g.shape, 0) % unit
    shift = 1
    while shift < unit:
        g = g + jnp.where(pos >= shift, pltpu.roll(g, shift, axis=0), 0.0)
        shift *= 2
    gc_ref[...] = g
    shift = 1
    while shift < unit:
        g = jnp.where((pos & shift) == 0, pltpu.roll(g, rows - shift, axis=0), g)
        shift *= 2
    gl_ref[...] = g


def _gates(x, w_tail, alog_row, dtb_row, *, unit, tm):
    T, D = x.shape
    spec = pl.BlockSpec((tm, LANE), lambda i: (i, 0))
    vec = pl.BlockSpec((1, LANE), lambda i: (0, 0))
    blocks = [_nbytes((tm, D), BF16), _nbytes((LANE, D), BF16), 3 * _nbytes((tm, LANE), F32)]
    return pl.pallas_call(
        functools.partial(_gates_kernel, unit=unit),
        grid=(T // tm,),
        in_specs=[pl.BlockSpec((tm, D), lambda i: (i, 0)), pl.BlockSpec((LANE, D), lambda i: (0, 0)), vec, vec],
        out_specs=[spec, spec, spec],
        out_shape=[jax.ShapeDtypeStruct((T, LANE), F32)] * 3,
        compiler_params=_params(1, blocks, 8 * _nbytes((tm, LANE), F32)),
        name="gates",
    )(x, w_tail, alog_row, dtb_row)


def _zmap(fn, *lists):
    return [fn(*args) for args in zip(*lists)]


def _block_terms(qs, ks, vs, cfs, his, grows, brows, masks, n_fac):
    incl, strict, eye = masks
    C, D = qs[0].shape
    bcols = [cf[:, CF_BETA + hi:CF_BETA + hi + 1] for cf, hi in zip(cfs, his)]
    gcols = [cf[:, CF_GC + hi:CF_GC + hi + 1] for cf, hi in zip(cfs, his)]
    glcols = [cf[:, CF_GLAST + hi:CF_GLAST + hi + 1] for cf, hi in zip(cfs, his)]
    decays = _zmap(lambda gc, gr: jnp.where(incl, jnp.exp(jnp.where(incl, gc - gr, 0.0)), 0.0), gcols, grows)
    qk2s = _zmap(lambda q, k: lax.dot_general(jnp.concatenate([q, k], axis=0), k, (((1,), (1,)), ((), ())),
                                              preferred_element_type=F32), qs, ks)
    a_s = _zmap(lambda qk2, dc, bc: jnp.where(strict, qk2[C:] * dc * bc, 0.0), qk2s, decays, bcols)
    ps = [jnp.where(eye, 1.0, 0.0) - a for a in a_s]
    if n_fac > 0:
        sqs = [_dot(a, a) for a in a_s]
        for f in range(n_fac):
            if f < n_fac - 1:
                rs = _zmap(lambda p, sq: _dot(jnp.concatenate([p, sq], axis=0), sq), ps, sqs)
                ps = _zmap(lambda p, r: p + r[:C], ps, rs)
                sqs = [r[C:] for r in rs]
            else:
                ps = _zmap(lambda p, sq: p + _dot(p, sq), ps, sqs)
    tbs = _zmap(lambda p, br: p * br, ps, brows)
    w1s = _zmap(lambda tb, gr, k: _dot(tb * jnp.exp(gr), k), tbs, grows, ks)
    u1s = _zmap(_dot, tbs, vs)
    wus = _zmap(lambda w1, u1: jnp.concatenate([w1, u1], axis=1), w1s, u1s)
    tops = _zmap(lambda qk2, dc, wu: _dot(jnp.where(incl, qk2[:C] * dc, 0.0), wu), qk2s, decays, wus)
    q_effs = _zmap(lambda q, gc, top: q * jnp.exp(gc) - top[:, :D], qs, gcols, tops)
    k_decs = _zmap(lambda k, gl, gc: k * jnp.exp(gl - gc), ks, glcols, gcols)
    return wus, q_effs, [top[:, D:] for top in tops], k_decs


def _conv_silu(xs, shifteds, w_ref, colss):
    ys = _zmap(lambda x, cols: x * w_ref[3:4, cols], xs, colss)
    for s in (1, 2, 3):
        ys = _zmap(lambda y, sh, cols: y + sh(s) * w_ref[3 - s:4 - s, cols], ys, shifteds, colss)
    return [y * jax.nn.sigmoid(y) for y in ys]


def _l2norm(x):
    return x * lax.rsqrt(jnp.sum(x * x, -1, keepdims=True) + L2_EPS)


def _finish(o, z, ong):
    o = o * lax.rsqrt(jnp.mean(o * o, -1, keepdims=True) + RMS_EPS) * ong
    return o * (z * jax.nn.sigmoid(z))


def _masks(C, unit):
    ri = lax.broadcasted_iota(jnp.int32, (C, C), 0)
    ci = lax.broadcasted_iota(jnp.int32, (C, C), 1)
    same = (ri // unit) == (ci // unit)
    return same & (ri >= ci), same & (ri > ci), ri == ci


def _delta_seq_kernel(q_ref, k_ref, v_ref, z_ref, pq_ref, pk_ref, pv_ref, hq_ref, hk_ref, hv_ref,
                      wq_ref, wk_ref, wv_ref, gch_ref, bth_ref, cf_ref, ong_ref, o_ref, s_ref,
                      lhs_scr, n_scr, o0_scr, *, hb, L, unroll):
    C, D = CHUNK, LANE
    N = L // C
    masks = _masks(C, C)
    n_fac = int(math.log2(C)) - 1
    ong = ong_ref[...]
    first_tile = pl.program_id(2) == 0

    @pl.when(first_tile)
    def _():
        for hi in range(hb):
            s_ref[0, hi] = jnp.zeros((D, D), F32)

    def pass1(g, carry):
        items = [(g * unroll + u, hi) for u in range(unroll) for hi in range(hb)]
        ns = [n for n, _ in items]
        his = [hi for _, hi in items]
        colss = [slice(hi * D, (hi + 1) * D) for hi in his]
        r0s = [pl.multiple_of(n * C, C) for n in ns]
        rps = [pl.multiple_of(jnp.maximum(r0 - CONV_HALO, 0), SUBLANE) for r0 in r0s]

        def conv(x_ref, p_ref, h_ref, w_ref):
            xs = _zmap(lambda r0, cols: x_ref[pl.ds(r0, C), cols], r0s, colss)
            prevs = _zmap(lambda n, rp, cols: jnp.where(
                n > 0, x_ref[pl.ds(rp, CONV_HALO), cols],
                jnp.where(first_tile, h_ref[0, :, cols], p_ref[:, cols])), ns, rps, colss)
            exts = _zmap(lambda prev, x: jnp.concatenate([prev, x], axis=0), prevs, xs)
            shifteds = [lambda s, ext=ext: pltpu.roll(ext, s, axis=0)[CONV_HALO:] for ext in exts]
            return _conv_silu(xs, shifteds, w_ref, colss)

        qs = [_l2norm(y) * (D ** -0.5) for y in conv(q_ref, pq_ref, hq_ref, wq_ref)]
        ks = [_l2norm(y) for y in conv(k_ref, pk_ref, hk_ref, wk_ref)]
        vs = conv(v_ref, pv_ref, hv_ref, wv_ref)
        cfs = [cf_ref[0, pl.ds(r0, C), :] for r0 in r0s]
        grows = [gch_ref[0, hi, pl.ds(n, 1), :] for n, hi in items]
        brows = [bth_ref[0, hi, pl.ds(n, 1), :] for n, hi in items]
        wus, q_effs, o0s, k_decs = _block_terms(qs, ks, vs, cfs, his, grows, brows, masks, n_fac)
        mns = _zmap(_dot_t0, k_decs, wus)
        for (n, hi), mn, q_eff, o0 in zip(items, mns, q_effs, o0s):
            lhs_scr[hi, n, :D, :] = mn[:, :D]
            lhs_scr[hi, n, D:, :] = q_eff
            n_scr[hi, n] = mn[:, D:]
            o0_scr[hi, n] = o0
        return carry

    if N == unroll:
        pass1(0, 0)
    else:
        lax.fori_loop(0, N // unroll, pass1, 0)

    def pass2(n, carry):
        r0 = pl.multiple_of(n * C, C)
        heads = list(range(hb))
        s_olds = [s_ref[0, hi] for hi in heads]
        rs = [_dot(lhs_scr[hi, n], s_old) for hi, s_old in zip(heads, s_olds)]
        for hi, s_old, r in zip(heads, s_olds, rs):
            glast = gch_ref[0, hi, pl.ds(n, 1), :][:, C - 1:C]
            s_ref[0, hi] = s_old * jnp.exp(glast) - r[:D] + n_scr[hi, n]
        for hi, r in zip(heads, rs):
            cols = slice(hi * D, (hi + 1) * D)
            o = o0_scr[hi, n] + r[D:]
            o_ref[pl.ds(r0, C), cols] = _finish(o, z_ref[pl.ds(r0, C), cols], ong).astype(o_ref.dtype)
        return carry

    lax.fori_loop(0, N, pass2, 0)


def _delta_units_kernel(q_ref, k_ref, v_ref, z_ref, hq_ref, hk_ref, hv_ref, wq_ref, wk_ref, wv_ref,
                        gch_ref, bth_ref, cf_ref, s0_ref, ong_ref, o_ref, s_ref, *, hb, unit):
    C, D = CHUNK, LANE
    n_units = C // unit
    masks = _masks(C, unit)
    n_fac = max(int(math.log2(unit)) - 1, 0)
    ong = ong_ref[...]
    pos = lax.broadcasted_iota(jnp.int32, (C, 1), 0) % unit
    his = list(range(hb))
    colss = [slice(hi * D, (hi + 1) * D) for hi in his]

    def conv(x_ref, h_ref, w_ref):
        xs = [x_ref[:, cols] for cols in colss]
        halos = [h_ref[:, cols] for cols in colss]
        shifteds = [lambda s, x=x, halo=halo: jnp.where(pos >= s, pltpu.roll(x, s, axis=0),
                                                        pltpu.roll(halo, C + s - CONV_HALO, axis=0))
                    for x, halo in zip(xs, halos)]
        return _conv_silu(xs, shifteds, w_ref, colss)

    qs = [_l2norm(y) * (D ** -0.5) for y in conv(q_ref, hq_ref, wq_ref)]
    ks = [_l2norm(y) for y in conv(k_ref, hk_ref, wk_ref)]
    vs = conv(v_ref, hv_ref, wv_ref)
    cf = cf_ref[0]
    wus, q_effs, o0s, k_decs = _block_terms(qs, ks, vs, [cf] * hb, his, [gch_ref[0, hi] for hi in his],
                                            [bth_ref[0, hi] for hi in his], masks, n_fac)
    units = [(hi, u) for u in range(n_units) for hi in his]
    rowss = [slice(u * unit, (u + 1) * unit) for _, u in units]
    s_olds = [s0_ref[u, hi] for hi, u in units]
    xs = [_dot(jnp.concatenate([wus[hi][rows, :D], q_effs[hi][rows]], axis=0), s_old)
          for (hi, _), rows, s_old in zip(units, rowss, s_olds)]
    upds = [_dot_t0(k_decs[hi][rows], wus[hi][rows, D:] - x[:unit]) for (hi, _), rows, x in zip(units, rowss, xs)]
    for (hi, u), s_old, upd in zip(units, s_olds, upds):
        glast = cf[u * unit:u * unit + 1, CF_GLAST + hi:CF_GLAST + hi + 1]
        s_ref[u, hi] = s_old * jnp.exp(glast) + upd
    for hi in his:
        o = jnp.concatenate([o0s[hi][rows] + x[unit:] for (h2, _), rows, x in zip(units, rowss, xs) if h2 == hi],
                            axis=0)
        o_ref[:, colss[hi]] = _finish(o, z_ref[:, colss[hi]], ong).astype(o_ref.dtype)


def _delta_seq(proj, offs, conv_halo, w_conv8, gc_hm, beta_hm, cf, o_norm_g, *, hb, lt, unroll):
    T = proj.shape[0]
    B, H, N, _ = gc_hm.shape
    D = LANE
    W = hb * D
    L = T // B
    n_t = L // lt
    nt = lt // CHUNK
    hpt = lt // CONV_HALO

    def col_spec(off):
        return pl.BlockSpec((lt, W), lambda i, j, t, o=off // W: (i * n_t + t, o + j))

    def prev_spec(off):
        return pl.BlockSpec((CONV_HALO, W),
                            lambda i, j, t, o=off // W: (jnp.maximum((i * n_t + t) * hpt - 1, 0), o + j))

    def halo_spec(g):
        return pl.BlockSpec((1, CONV_HALO, W), lambda i, j, t, o=g * (H // hb): (i, 0, o + j))

    def w_spec(g):
        return pl.BlockSpec((CONV_HALO, W), lambda i, j, t, o=g * (H // hb): (0, o + j))

    hm_spec = pl.BlockSpec((1, hb, nt, CHUNK), lambda i, j, t: (i, j, t, 0))
    state_spec = pl.BlockSpec((1, hb, D, D), lambda i, j, t: (i, j, 0, 0))
    in_specs = ([col_spec(o) for o in offs] + [prev_spec(o) for o in offs[:3]] + [halo_spec(g) for g in range(3)]
                + [w_spec(g) for g in range(3)]
                + [hm_spec, hm_spec, pl.BlockSpec((1, lt, LANE), lambda i, j, t: (j, i * n_t + t, 0)),
                   pl.BlockSpec((1, D), lambda i, j, t: (0, 0))])
    args = [proj] * 7 + [conv_halo] * 3 + [w_conv8] * 3 + [gc_hm, beta_hm, cf, o_norm_g.reshape(1, D)]
    blocks = [4 * _nbytes((lt, W), F32), 9 * _nbytes((CONV_HALO, W), F32),
              2 * _nbytes((hb, max(nt, SUBLANE), LANE), F32), _nbytes((lt, LANE), F32),
              _nbytes((hb, D, D), F32), _nbytes((lt, W), BF16)]
    scratch = [pltpu.VMEM((hb, nt, D + CHUNK, D), F32), pltpu.VMEM((hb, nt, D, D), F32),
               pltpu.VMEM((hb, nt, CHUNK, D), F32)]
    return pl.pallas_call(
        functools.partial(_delta_seq_kernel, hb=hb, L=lt, unroll=unroll),
        grid=(B, H // hb, n_t),
        in_specs=in_specs,
        out_specs=[pl.BlockSpec((lt, W), lambda i, j, t: (i * n_t + t, j)), state_spec],
        out_shape=[jax.ShapeDtypeStruct((T, H * D), BF16), jax.ShapeDtypeStruct((B, H, D, D), F32)],
        scratch_shapes=scratch,
        compiler_params=_params(3, blocks, hb * nt * _nbytes((2 * D + 2 * CHUNK, D), F32)),
        name="delta_seq",
    )(*args)


def _delta_units(proj, offs, conv_halo, w_conv8, gc_hm, beta_hm, cf, s0, o_norm_g, *, unit, hb):
    T = proj.shape[0]
    nb, H, _, _ = gc_hm.shape
    D = LANE
    W = hb * D
    n_units = CHUNK // unit

    def col_spec(off):
        return pl.BlockSpec((CHUNK, W), lambda i, j, o=off // W: (i, o + j))

    def halo_spec(g):
        return pl.BlockSpec((CHUNK, W), lambda i, j, o=g * (H // hb): (i, o + j))

    def w_spec(g):
        return pl.BlockSpec((CONV_HALO, W), lambda i, j, o=g * (H // hb): (0, o + j))

    hm_spec = pl.BlockSpec((1, hb, 1, CHUNK), lambda i, j: (i, j, 0, 0))
    state_spec = pl.BlockSpec((n_units, hb, D, D), lambda i, j: (i, j, 0, 0))
    in_specs = ([col_spec(o) for o in offs] + [halo_spec(g) for g in range(3)] + [w_spec(g) for g in range(3)]
                + [hm_spec, hm_spec, pl.BlockSpec((1, CHUNK, LANE), lambda i, j: (j, i, 0)), state_spec,
                   pl.BlockSpec((1, D), lambda i, j: (0, 0))])
    args = [proj] * 4 + [conv_halo] * 3 + [w_conv8] * 3 + [gc_hm, beta_hm, cf, s0, o_norm_g.reshape(1, D)]
    blocks = [7 * _nbytes((CHUNK, W), F32), 3 * _nbytes((CONV_HALO, W), F32),
              2 * _nbytes((hb, SUBLANE, LANE), F32), _nbytes((CHUNK, LANE), F32),
              2 * _nbytes((n_units, hb, D, D), F32), _nbytes((CHUNK, W), BF16)]
    return pl.pallas_call(
        functools.partial(_delta_units_kernel, hb=hb, unit=unit),
        grid=(nb, H // hb),
        in_specs=in_specs,
        out_specs=[pl.BlockSpec((CHUNK, W), lambda i, j: (i, j)), state_spec],
        out_shape=[jax.ShapeDtypeStruct((T, H * D), BF16),
                   jax.ShapeDtypeStruct((nb * n_units, H, D, D), F32)],
        compiler_params=_params(2, blocks),
        name="delta_units",
    )(*args)


def _head_major(tok, nb, H, lane0):
    T = tok.shape[0]
    x = tok[:, lane0:lane0 + H].reshape(nb, T // nb // CHUNK, CHUNK, H)
    return jnp.transpose(x, (0, 3, 1, 2))


def _col_form(beta_tok, gc_tok, gl_tok, H, hb):
    T = beta_tok.shape[0]

    def part(tok, lane0, width):
        x = tok[:, lane0:lane0 + H].reshape(T, H // hb, hb)
        x = jnp.transpose(x, (1, 0, 2))
        return jnp.pad(x, ((0, 0), (0, 0), (0, width - hb)))

    return jnp.concatenate([part(beta_tok, 0, CF_GC - CF_BETA), part(gc_tok, H, CF_GLAST - CF_GC),
                            part(gl_tok, H, LANE - CF_GLAST)], axis=-1)


def _tail_rows(state, new, n):
    L = new.shape[1]
    if L >= n:
        return new[:, L - n:]
    return jnp.concatenate([state[:, state.shape[1] - (n - L):], new], axis=1)


def _layer(x, p, pool_state, conv_state, delta_state, start, lw, cfg):
    (w_in_t, w_tail_b, w_pool_b, pool_scale, w_conv8, alog_row, dtb_row, o_norm_g, w_out, ln1_g, ln1_b,
     w_gate_up, w_down_b, ln2_g, ln2_b, w_ple_gate, w_ple_proj_b, alpha) = lw
    B, L, D = x.shape
    T = B * L
    P, DN, H, d_ff = cfg["P"], cfg["DN"], cfg["H"], cfg["d_ff"]
    n_main = P + 4 * DN
    tm = _pick(T, (1024, 512, 256, 128))
    tm_big = _pick(T, (2048, 1024, 512, 256, 128))
    tm_ln = min(tm, 256)
    tm_down = min(tm, 512)
    tn_d = _pick(D, (512, 256, 128))
    tn_ff = _pick(d_ff, (256, 128))
    x2 = x.reshape(T, D)
    xb = x2.astype(BF16)

    proj = _matmul_nt(xb, w_in_t, n_main, tm=_pick(T, (2048, 1024, 512, 256, 128)), tn=_pick(n_main, (256, 128)),
                      out_dtype=F32)
    proj3 = proj.reshape(B, L, n_main)
    n_pool, n_conv = pool_state.shape[1], conv_state.shape[1]
    new_pool = _tail_rows(pool_state, proj3[:, :, :P], n_pool)
    new_conv = _tail_rows(conv_state, proj3[:, :, P:P + 3 * DN], n_conv)

    pool_halo = jnp.pad(pool_state, ((0, 0), (POOL_HALO - n_pool, 0), (0, 0)))
    bb_pool = 1 if L >= 256 else _pick(B, (16, 8, 4, 2, 1))
    y_pool = _pool_mixer(proj, pool_halo, w_pool_b, pool_scale, bb=bb_pool, L=L, start=start)

    unit = math.gcd(L, CHUNK)
    beta_tok, gc_tok, gl_tok = _gates(xb, w_tail_b, alog_row, dtb_row, unit=unit, tm=tm)
    conv_halo = jnp.pad(conv_state, ((0, 0), (CONV_HALO - n_conv, 0), (0, 0)))
    offs = (P, P + DN, P + 2 * DN, P + 3 * DN)
    if unit == CHUNK:
        assert delta_state is None
        hb = _pick(H, (8, 4, 2, 1))
        o, s_new = _delta_seq(proj, offs, conv_halo, w_conv8, _head_major(gc_tok, B, H, H),
                              _head_major(beta_tok, B, H, 0), _col_form(beta_tok, gc_tok, gl_tok, H, hb),
                              o_norm_g, hb=hb, lt=_pick(L, (512, 256, 128, 64)), unroll=2)
    else:
        assert unit == CONV_HALO and T % CHUNK == 0
        nb, hb = T // CHUNK, _pick(H, (8, 4, 2, 1))
        o, s_new = _delta_units(proj, offs, conv_halo.reshape(B * CONV_HALO, 3 * DN), w_conv8,
                                _head_major(gc_tok, nb, H, H), _head_major(beta_tok, nb, H, 0),
                                _col_form(beta_tok, gc_tok, gl_tok, H, hb), delta_state, o_norm_g,
                                unit=unit, hb=hb)

    s1 = _matmul_resid((y_pool, o), w_out, x2, alpha, tm=tm_big, tn=tn_ff)
    h1, h1b = _layer_norm(s1, ln1_g, ln1_b, tm=tm_ln)
    act = _swiglu(h1b, w_gate_up, d_ff, tm=tm, tn=tn_ff)
    s2 = _matmul_resid((act,), w_down_b, h1, alpha, tm=tm_down, tn=tn_ff)
    h2, h2b = _layer_norm(s2, ln2_g, ln2_b, tm=tm_ln)
    y = _ple(h2b, h2, p.reshape(T, -1).astype(BF16), w_ple_gate, w_ple_proj_b, tm=tm_big, tn=tn_ff)
    return y.reshape(B, L, D), new_pool, new_conv, s_new


def kernel(x_prompt, x_sample, state_pool, state_conv, state_delta, p_prompt, p_sample, w_in, w_pool, pool_scale, w_conv, a_log, dt_bias, o_norm_g, w_out, ln1_g, ln1_b, w_gate_up, w_down, ln2_g, ln2_b, w_ple_gate, w_ple_proj):
    depth = w_in.shape[0]
    B = x_prompt.shape[0]
    D = x_prompt.shape[2]
    P = w_pool.shape[1] * w_pool.shape[2]
    H = a_log.shape[1]
    DN = H * LANE
    n_main = P + 4 * DN
    cfg = dict(P=P, DN=DN, H=H, d_ff=w_down.shape[1])
    alpha = (2.0 * depth) ** 0.25
    assert w_in.shape[2] == n_main + 2 * H and H <= CF_GC - CF_BETA and n_main % LANE == 0
    assert w_conv.shape[1] == 4 and D == P + DN and max(POOL_WINDOWS) - 1 == state_pool.shape[2]
    assert o_norm_g.shape[1] == LANE

    yp, ys = x_prompt, x_sample
    outs = [[] for _ in range(6)]
    for li in range(depth):
        w_in_t = jnp.swapaxes(w_in[li], 0, 1)
        lw = (w_in_t,
              jnp.pad(w_in_t[n_main:], ((0, LANE - 2 * H), (0, 0))).astype(BF16),
              w_pool[li].astype(BF16), pool_scale[li],
              jnp.pad(w_conv[li], ((0, CONV_HALO - w_conv.shape[1]), (0, 0))),
              jnp.pad(a_log[li], (H, LANE - 2 * H)).reshape(1, LANE),
              jnp.pad(dt_bias[li], (H, LANE - 2 * H)).reshape(1, LANE),
              o_norm_g[li], w_out[li], ln1_g[li], ln1_b[li], w_gate_up[li],
              w_down[li].astype(BF16), ln2_g[li], ln2_b[li], w_ple_gate[li],
              w_ple_proj[li].astype(BF16), alpha)
        zeros_pool = jnp.zeros((B,) + state_pool.shape[2:], F32)
        zeros_conv = jnp.zeros((B,) + state_conv.shape[2:], F32)
        yp, npl, ncv, nst = _layer(yp, p_prompt[li], zeros_pool, zeros_conv, None, 0, lw, cfg)
        ys, spl, scv, sst = _layer(ys, p_sample[li], state_pool[li], state_conv[li], state_delta[li], PAST_LEN,
                                   lw, cfg)
        for acc, val in zip(outs, (npl, ncv, nst, spl, scv, sst)):
            acc.append(val)
    return (yp, ys) + tuple(jnp.stack(o) for o in outs)
```

```python
import functools
import math

import jax
import jax.numpy as jnp
from jax import lax
from jax.experimental import pallas as pl
from jax.experimental.pallas import tpu as pltpu

F32 = jnp.float32
BF16 = jnp.bfloat16

PAST_LEN = 16384
POOL_WINDOWS = (2, 4, 8, 16)
POOL_HALO = 16
CONV_HALO = 8
CHUNK = 64
LN_EPS = 1e-5
RMS_EPS = 1e-6
L2_EPS = 1e-6

LANE = 128
SUBLANE = 8
VMEM_BUDGET = 56 * 1024 * 1024

CF_BETA, CF_GC, CF_GLAST = 0, 32, 64


def _vmem_limit(block_bytes, scratch_bytes=0):
    est = 2 * sum(block_bytes) + scratch_bytes + (12 << 20)
    return int(min(est, VMEM_BUDGET))


def _nbytes(shape, dtype):
    return math.prod(shape) * jnp.dtype(dtype).itemsize


def _pick(n, candidates):
    for c in candidates:
        if n % c == 0:
            return c
    raise ValueError(f"no tile for {n} among {candidates}")


def _params(n_axes, block_bytes, scratch_bytes=0):
    return pltpu.CompilerParams(dimension_semantics=("arbitrary",) * n_axes,
                                vmem_limit_bytes=_vmem_limit(block_bytes, scratch_bytes))


def _dot(a, b):
    return jnp.dot(a, b, preferred_element_type=F32)


def _dot_t0(a, b):
    return lax.dot_general(a, b, (((0,), (0,)), ((), ())), preferred_element_type=F32)


MM_SUB_ROWS = 1024


def _mm_nt_kernel(x_ref, wt_ref, o_ref):
    wt = wt_ref[...].astype(BF16)
    tm = x_ref.shape[0]
    sub = min(tm, MM_SUB_ROWS)
    for r in range(0, tm, sub):
        o_ref[r:r + sub, :] = lax.dot_general(x_ref[r:r + sub, :], wt, (((1,), (1,)), ((), ())),
                                              preferred_element_type=F32).astype(o_ref.dtype)


def _matmul_nt(x, wt, n_cols, *, tm, tn, out_dtype):
    T, K = x.shape
    blocks = [_nbytes((tm, K), x.dtype), _nbytes((tn, K), wt.dtype), _nbytes((tm, tn), out_dtype)]
    return pl.pallas_call(
        _mm_nt_kernel,
        grid=(T // tm, n_cols // tn),
        in_specs=[pl.BlockSpec((tm, K), lambda i, j: (i, 0)),
                  pl.BlockSpec((tn, K), lambda i, j: (j, 0))],
        out_specs=pl.BlockSpec((tm, tn), lambda i, j: (i, j)),
        out_shape=jax.ShapeDtypeStruct((T, n_cols), out_dtype),
        compiler_params=_params(2, blocks, _nbytes((tm, tn), F32) + _nbytes((tn, K), BF16)),
        name="matmul_nt",
    )(x, wt)


def _ln_apply(s, mu, rstd, g, b):
    reps = s.shape[1] // LANE
    mu = jnp.concatenate([mu] * reps, axis=1)
    rstd = jnp.concatenate([rstd] * reps, axis=1)
    return (s - mu) * rstd * g + b


def _ln_tile_specs(tm, tn):
    stat = pl.BlockSpec((tm, LANE), lambda i, j: (i, 0))
    vec = pl.BlockSpec((1, tn), lambda i, j: (0, j))
    return [pl.BlockSpec((tm, tn), lambda i, j: (i, j)), stat, stat, vec, vec]


def _mm_resid_kernel(*refs, alpha, splits, normed):
    x_refs, (w_ref, *r_refs, o_ref) = refs[:len(splits)], refs[len(splits):]
    w = w_ref[...].astype(BF16)
    tm = o_ref.shape[0]
    sub = min(tm, MM_SUB_ROWS)
    for r in range(0, tm, sub):
        rows = slice(r, r + sub)
        if normed:
            s_ref, mu_ref, rstd_ref, g_ref, b_ref = r_refs
            res = _ln_apply(s_ref[rows, :], mu_ref[rows, :], rstd_ref[rows, :], g_ref[...], b_ref[...])
        else:
            res = r_refs[0][rows, :]
        acc = alpha * res
        k0 = 0
        for x_ref, kw in zip(x_refs, splits):
            acc = acc + _dot(x_ref[rows, :], w[k0:k0 + kw, :])
            k0 += kw
        o_ref[rows, :] = acc


def _matmul_resid(xs, w, resid, alpha, *, tm, tn):
    T = xs[0].shape[0]
    splits = tuple(x.shape[1] for x in xs)
    K, N = w.shape
    normed = isinstance(resid, tuple)
    if normed:
        r_args, r_specs = list(resid), _ln_tile_specs(tm, tn)
    else:
        r_args, r_specs = [resid], [pl.BlockSpec((tm, tn), lambda i, j: (i, j))]
    blocks = [_nbytes((tm, K), BF16), _nbytes((K, tn), w.dtype), 2 * _nbytes((tm, tn), F32),
              2 * _nbytes((tm, LANE), F32)]
    return pl.pallas_call(
        functools.partial(_mm_resid_kernel, alpha=alpha, splits=splits, normed=normed),
        grid=(T // tm, N // tn),
        in_specs=[pl.BlockSpec((tm, kw), lambda i, j: (i, 0)) for kw in splits]
        + [pl.BlockSpec((K, tn), lambda i, j: (0, j))] + r_specs,
        out_specs=pl.BlockSpec((tm, tn), lambda i, j: (i, j)),
        out_shape=jax.ShapeDtypeStruct((T, N), F32),
        compiler_params=_params(2, blocks, _nbytes((tm, tn), F32)),
        name="matmul_resid",
    )(*xs, w, *r_args)


def _ln_kernel(s_ref, g_ref, b_ref, hb_ref, mu_ref, rstd_ref):
    s = s_ref[...]
    mu = jnp.mean(s, -1, keepdims=True)
    c = s - mu
    var = jnp.mean(c * c, -1, keepdims=True)
    rstd = lax.rsqrt(var + LN_EPS)
    hb_ref[...] = (c * rstd * g_ref[...] + b_ref[...]).astype(BF16)
    mu_ref[...] = jnp.broadcast_to(mu, mu_ref.shape)
    rstd_ref[...] = jnp.broadcast_to(rstd, rstd_ref.shape)


def _layer_norm(s, g, b, *, tm):
    T, D = s.shape
    blocks = [_nbytes((tm, D), F32), _nbytes((tm, D), BF16), 2 * _nbytes((1, D), F32), 2 * _nbytes((tm, LANE), F32)]
    row = pl.BlockSpec((tm, D), lambda i: (i, 0))
    vec = pl.BlockSpec((1, D), lambda i: (0, 0))
    stat = pl.BlockSpec((tm, LANE), lambda i: (i, 0))
    g2, b2 = g.reshape(1, D), b.reshape(1, D)
    hb, mu, rstd = pl.pallas_call(
        _ln_kernel,
        grid=(T // tm,),
        in_specs=[row, vec, vec],
        out_specs=[row, stat, stat],
        out_shape=[jax.ShapeDtypeStruct((T, D), BF16)] + [jax.ShapeDtypeStruct((T, LANE), F32)] * 2,
        compiler_params=_params(1, blocks, 2 * _nbytes((tm, D), F32)),
        name="layer_norm",
    )(s, g2, b2)
    return hb, (s, mu, rstd, g2, b2)


def _swiglu_kernel(h_ref, wg_ref, wu_ref, o_ref):
    h = h_ref[...]
    gate = _dot(h, wg_ref[...].astype(BF16))
    up = _dot(h, wu_ref[...].astype(BF16))
    o_ref[...] = (gate * jax.nn.sigmoid(gate) * up).astype(o_ref.dtype)


def _swiglu(h, w_gate_up, d_ff, *, tm, tn):
    T, K = h.shape
    n_j = d_ff // tn
    blocks = [_nbytes((tm, K), BF16), 2 * _nbytes((K, tn), w_gate_up.dtype), _nbytes((tm, tn), BF16)]
    return pl.pallas_call(
        _swiglu_kernel,
        grid=(T // tm, n_j),
        in_specs=[pl.BlockSpec((tm, K), lambda i, j: (i, 0)),
                  pl.BlockSpec((K, tn), lambda i, j: (0, j)),
                  pl.BlockSpec((K, tn), lambda i, j: (0, j + n_j))],
        out_specs=pl.BlockSpec((tm, tn), lambda i, j: (i, j)),
        out_shape=jax.ShapeDtypeStruct((T, d_ff), BF16),
        compiler_params=_params(2, blocks, 3 * _nbytes((tm, tn), F32)),
        name="swiglu",
    )(h, w_gate_up, w_gate_up)


def _ple_kernel(hb_ref, wg_ref, p_ref, wp_ref, s_ref, mu_ref, rstd_ref, g_ref, b_ref, y_ref):
    wg = wg_ref[...].astype(BF16)
    wp = wp_ref[...]
    tm = y_ref.shape[0]
    sub = min(tm, MM_SUB_ROWS)
    for r in range(0, tm, sub):
        rows = slice(r, r + sub)
        h = _ln_apply(s_ref[rows, :], mu_ref[rows, :], rstd_ref[rows, :], g_ref[...], b_ref[...])
        gate = jax.nn.sigmoid(_dot(hb_ref[rows, :], wg))
        y_ref[rows, :] = h + gate * _dot(p_ref[rows, :], wp)


def _ple(hb, h_ln, p, w_gate, w_proj, *, tm, tn):
    T, D = hb.shape
    P = p.shape[1]
    blocks = [_nbytes((tm, D), BF16), _nbytes((D, tn), w_gate.dtype), _nbytes((tm, P), BF16),
              _nbytes((P, tn), BF16), 2 * _nbytes((tm, tn), F32), 2 * _nbytes((tm, LANE), F32)]
    return pl.pallas_call(
        _ple_kernel,
        grid=(T // tm, D // tn),
        in_specs=[pl.BlockSpec((tm, D), lambda i, j: (i, 0)),
                  pl.BlockSpec((D, tn), lambda i, j: (0, j)),
                  pl.BlockSpec((tm, P), lambda i, j: (i, 0)),
                  pl.BlockSpec((P, tn), lambda i, j: (0, j))] + _ln_tile_specs(tm, tn),
        out_specs=pl.BlockSpec((tm, tn), lambda i, j: (i, j)),
        out_shape=jax.ShapeDtypeStruct((T, D), F32),
        compiler_params=_params(2, blocks, 3 * _nbytes((tm, tn), F32)),
        name="ple",
    )(hb, w_gate, p, w_proj, *h_ln)


def _pool_kernel(u_ref, halo_ref, w_ref, scale_ref, o_ref, d_ref, *, bb, L, tl, start):
    P = u_ref.shape[1]
    G = len(POOL_WINDOWS)
    gw = P // G
    n_sub = L // tl
    row = lax.broadcasted_iota(jnp.int32, (tl, 1), 0)

    def sub_tile(bi, t, tail):
        r0 = pl.multiple_of(bi * L + t * tl, SUBLANE)
        x = u_ref[pl.ds(r0, tl), :]
        ext = jnp.concatenate([tail, x], axis=0)
        pos = start + t * tl + row
        for gi, win in enumerate(POOL_WINDOWS):
            cols = slice(gi * gw, (gi + 1) * gw)
            s = ext[:, cols]
            shift = 1
            while shift < win:
                s = s + pltpu.roll(s, shift, axis=0)
                shift *= 2
            cnt = jnp.minimum(pos + 1, win).astype(F32)
            d = s[POOL_HALO:] / cnt - x[:, cols]
            d_ref[pl.ds(r0, tl), cols] = d
        return ext[tl:]

    def seq(bi, carry):
        tail0 = halo_ref[bi]
        if n_sub == 1:
            sub_tile(bi, 0, tail0)
        else:
            lax.fori_loop(0, n_sub, lambda t, tail: sub_tile(bi, t, tail), tail0)
        return carry

    lax.fori_loop(0, bb, seq, 0)
    for gi in range(G):
        cols = slice(gi * gw, (gi + 1) * gw)
        y = _dot(d_ref[:, cols].astype(BF16), w_ref[gi])
        o_ref[:, cols] = (y * scale_ref[:, cols]).astype(o_ref.dtype)


def _pool_mixer(proj, halo, w_pool, pool_scale, *, bb, L, start):
    B, _, P = halo.shape
    G, gw, _ = w_pool.shape
    rows = bb * L
    tl = min(L, 256)
    blocks = [_nbytes((rows, P), F32), _nbytes((bb, POOL_HALO, P), F32), _nbytes(w_pool.shape, BF16),
              _nbytes((1, P), F32), _nbytes((rows, P), BF16)]
    return pl.pallas_call(
        functools.partial(_pool_kernel, bb=bb, L=L, tl=tl, start=start),
        grid=(B // bb,),
        in_specs=[pl.BlockSpec((rows, P), lambda i: (i, 0)),
                  pl.BlockSpec((bb, POOL_HALO, P), lambda i: (i, 0, 0)),
                  pl.BlockSpec((G, gw, gw), lambda i: (0, 0, 0)),
                  pl.BlockSpec((1, P), lambda i: (0, 0))],
        out_specs=pl.BlockSpec((rows, P), lambda i: (i, 0)),
        out_shape=jax.ShapeDtypeStruct((B * L, P), BF16),
        scratch_shapes=[pltpu.VMEM((rows, P), F32)],
        compiler_params=_params(1, blocks, _nbytes((rows, P), F32) + 4 * _nbytes((tl + POOL_HALO, P), F32)),
        name="pool_mixer",
    )(proj, halo, w_pool, pool_scale.reshape(1, P))


def _gates_kernel(x_ref, w_ref, alog_ref, dtb_ref, xb_ref, beta_ref, gc_ref, gl_ref, *, unit):
    xb = x_ref[...].astype(BF16)
    xb_ref[...] = xb
    t = lax.dot_general(xb, w_ref[...], (((1,), (1,)), ((), ())),
                        preferred_element_type=F32)
    beta_ref[...] = jax.nn.sigmoid(t)
    x = t + dtb_ref[...]
    softplus = jnp.maximum(x, 0.0) + jnp.log1p(jnp.exp(-jnp.abs(x)))
    g = -jnp.exp(alog_ref[...]) * softplus
    rows = g.shape[0]
    pos = lax.broadcasted_iota(jnp.int32, g.shape, 0) % unit
    shift = 1
    while shift < unit:
        g = g + jnp.where(pos >= shift, pltpu.roll(g, shift, axis=0), 0.0)
        shift *= 2
    gc_ref[...] = g
    shift = 1
    while shift < unit:
        g = jnp.where((pos & shift) == 0, pltpu.roll(g, rows - shift, axis=0), g)
        shift *= 2
    gl_ref[...] = g


def _gates(x, w_tail, alog_row, dtb_row, *, unit, tm):
    T, D = x.shape
    spec = pl.BlockSpec((tm, LANE), lambda i: (i, 0))
    vec = pl.BlockSpec((1, LANE), lambda i: (0, 0))
    row = pl.BlockSpec((tm, D), lambda i: (i, 0))
    blocks = [_nbytes((tm, D), F32), _nbytes((tm, D), BF16), _nbytes((LANE, D), BF16), 3 * _nbytes((tm, LANE), F32)]
    return pl.pallas_call(
        functools.partial(_gates_kernel, unit=unit),
        grid=(T // tm,),
        in_specs=[row, pl.BlockSpec((LANE, D), lambda i: (0, 0)), vec, vec],
        out_specs=[row, spec, spec, spec],
        out_shape=[jax.ShapeDtypeStruct((T, D), BF16)] + [jax.ShapeDtypeStruct((T, LANE), F32)] * 3,
        compiler_params=_params(1, blocks, 8 * _nbytes((tm, LANE), F32)),
        name="gates",
    )(x, w_tail, alog_row, dtb_row)


def _zmap(fn, *lists):
    return [fn(*args) for args in zip(*lists)]


def _block_terms(qs, ks, vs, cfs, his, grows, brows, masks, n_fac):
    incl, strict, eye = masks
    C, D = qs[0].shape
    bcols = [cf[:, CF_BETA + hi:CF_BETA + hi + 1] for cf, hi in zip(cfs, his)]
    gcols = [cf[:, CF_GC + hi:CF_GC + hi + 1] for cf, hi in zip(cfs, his)]
    glcols = [cf[:, CF_GLAST + hi:CF_GLAST + hi + 1] for cf, hi in zip(cfs, his)]
    decays = _zmap(lambda gc, gr: jnp.where(incl, jnp.exp(jnp.where(incl, gc - gr, 0.0)), 0.0), gcols, grows)
    qk2s = _zmap(lambda q, k: lax.dot_general(jnp.concatenate([q, k], axis=0), k, (((1,), (1,)), ((), ())),
                                              preferred_element_type=F32), qs, ks)
    a_s = _zmap(lambda qk2, dc, bc: jnp.where(strict, qk2[C:] * dc * bc, 0.0), qk2s, decays, bcols)
    ps = [jnp.where(eye, 1.0, 0.0) - a for a in a_s]
    if n_fac > 0:
        sqs = [_dot(a, a) for a in a_s]
        for f in range(n_fac):
            if f < n_fac - 1:
                rs = _zmap(lambda p, sq: _dot(jnp.concatenate([p, sq], axis=0), sq), ps, sqs)
                ps = _zmap(lambda p, r: p + r[:C], ps, rs)
                sqs = [r[C:] for r in rs]
            else:
                ps = _zmap(lambda p, sq: p + _dot(p, sq), ps, sqs)
    tbs = _zmap(lambda p, br: p * br, ps, brows)
    w1s = _zmap(lambda tb, gr, k: _dot(tb * jnp.exp(gr), k), tbs, grows, ks)
    u1s = _zmap(_dot, tbs, vs)
    wus = _zmap(lambda w1, u1: jnp.concatenate([w1, u1], axis=1), w1s, u1s)
    tops = _zmap(lambda qk2, dc, wu: _dot(jnp.where(incl, qk2[:C] * dc, 0.0), wu), qk2s, decays, wus)
    q_effs = _zmap(lambda q, gc, top: q * jnp.exp(gc) - top[:, :D], qs, gcols, tops)
    k_decs = _zmap(lambda k, gl, gc: k * jnp.exp(gl - gc), ks, glcols, gcols)
    return wus, q_effs, [top[:, D:] for top in tops], k_decs


def _conv_silu(xs, shifteds, w_ref, colss):
    ys = _zmap(lambda x, cols: x * w_ref[3:4, cols], xs, colss)
    for s in (1, 2, 3):
        ys = _zmap(lambda y, sh, cols: y + sh(s) * w_ref[3 - s:4 - s, cols], ys, shifteds, colss)
    return [y * jax.nn.sigmoid(y) for y in ys]


def _l2norm(x):
    return x * lax.rsqrt(jnp.sum(x * x, -1, keepdims=True) + L2_EPS)


def _finish(o, z, ong):
    o = o * lax.rsqrt(jnp.mean(o * o, -1, keepdims=True) + RMS_EPS) * ong
    return o * (z * jax.nn.sigmoid(z))


def _masks(C, unit):
    ri = lax.broadcasted_iota(jnp.int32, (C, C), 0)
    ci = lax.broadcasted_iota(jnp.int32, (C, C), 1)
    same = (ri // unit) == (ci // unit)
    return same & (ri >= ci), same & (ri > ci), ri == ci


def _delta_seq_kernel(q_ref, k_ref, v_ref, z_ref, pq_ref, pk_ref, pv_ref, hq_ref, hk_ref, hv_ref,
                      wq_ref, wk_ref, wv_ref, gch_ref, bth_ref, cf_ref, ong_ref, o_ref, s_ref,
                      lhs_scr, n_scr, o0_scr, *, hb, L, unroll):
    C, D = CHUNK, LANE
    N = L // C
    masks = _masks(C, C)
    n_fac = int(math.log2(C)) - 1
    ong = ong_ref[...]
    first_tile = pl.program_id(2) == 0

    @pl.when(first_tile)
    def _():
        for hi in range(hb):
            s_ref[0, hi] = jnp.zeros((D, D), F32)

    def pass1(g, carry):
        items = [(g * unroll + u, hi) for u in range(unroll) for hi in range(hb)]
        ns = [n for n, _ in items]
        his = [hi for _, hi in items]
        colss = [slice(hi * D, (hi + 1) * D) for hi in his]
        r0s = [pl.multiple_of(n * C, C) for n in ns]
        rps = [pl.multiple_of(jnp.maximum(r0 - CONV_HALO, 0), SUBLANE) for r0 in r0s]

        def conv(x_ref, p_ref, h_ref, w_ref):
            xs = _zmap(lambda r0, cols: x_ref[pl.ds(r0, C), cols], r0s, colss)
            prevs = _zmap(lambda n, rp, cols: jnp.where(
                n > 0, x_ref[pl.ds(rp, CONV_HALO), cols],
                jnp.where(first_tile, h_ref[0, :, cols], p_ref[:, cols])), ns, rps, colss)
            exts = _zmap(lambda prev, x: jnp.concatenate([prev, x], axis=0), prevs, xs)
            shifteds = [lambda s, ext=ext: pltpu.roll(ext, s, axis=0)[CONV_HALO:] for ext in exts]
            return _conv_silu(xs, shifteds, w_ref, colss)

        qs = [_l2norm(y) * (D ** -0.5) for y in conv(q_ref, pq_ref, hq_ref, wq_ref)]
        ks = [_l2norm(y) for y in conv(k_ref, pk_ref, hk_ref, wk_ref)]
        vs = conv(v_ref, pv_ref, hv_ref, wv_ref)
        cfs = [cf_ref[0, pl.ds(r0, C), :] for r0 in r0s]
        grows = [gch_ref[0, hi, pl.ds(n, 1), :] for n, hi in items]
        brows = [bth_ref[0, hi, pl.ds(n, 1), :] for n, hi in items]
        wus, q_effs, o0s, k_decs = _block_terms(qs, ks, vs, cfs, his, grows, brows, masks, n_fac)
        mns = _zmap(_dot_t0, k_decs, wus)
        for (n, hi), mn, q_eff, o0 in zip(items, mns, q_effs, o0s):
            lhs_scr[hi, n, :D, :] = mn[:, :D]
            lhs_scr[hi, n, D:, :] = q_eff
            n_scr[hi, n] = mn[:, D:]
            o0_scr[hi, n] = o0
        return carry

    if N == unroll:
        pass1(0, 0)
    else:
        lax.fori_loop(0, N // unroll, pass1, 0)

    def pass2(n, carry):
        r0 = pl.multiple_of(n * C, C)
        heads = list(range(hb))
        s_olds = [s_ref[0, hi] for hi in heads]
        rs = [_dot(lhs_scr[hi, n], s_old) for hi, s_old in zip(heads, s_olds)]
        for hi, s_old, r in zip(heads, s_olds, rs):
            glast = gch_ref[0, hi, pl.ds(n, 1), :][:, C - 1:C]
            s_ref[0, hi] = s_old * jnp.exp(glast) - r[:D] + n_scr[hi, n]
        for hi, r in zip(heads, rs):
            cols = slice(hi * D, (hi + 1) * D)
            o = o0_scr[hi, n] + r[D:]
            o_ref[pl.ds(r0, C), cols] = _finish(o, z_ref[pl.ds(r0, C), cols], ong).astype(o_ref.dtype)
        return carry

    lax.fori_loop(0, N, pass2, 0)


def _delta_units_kernel(q_ref, k_ref, v_ref, z_ref, hq_ref, hk_ref, hv_ref, wq_ref, wk_ref, wv_ref,
                        gch_ref, bth_ref, cf_ref, s0_ref, ong_ref, o_ref, s_ref, *, hb, unit):
    C, D = CHUNK, LANE
    n_units = C // unit
    masks = _masks(C, unit)
    n_fac = max(int(math.log2(unit)) - 1, 0)
    ong = ong_ref[...]
    pos = lax.broadcasted_iota(jnp.int32, (C, 1), 0) % unit
    his = list(range(hb))
    colss = [slice(hi * D, (hi + 1) * D) for hi in his]

    def conv(x_ref, h_ref, w_ref):
        xs = [x_ref[:, cols] for cols in colss]
        halos = [h_ref[:, cols] for cols in colss]
        shifteds = [lambda s, x=x, halo=halo: jnp.where(pos >= s, pltpu.roll(x, s, axis=0),
                                                        pltpu.roll(halo, C + s - CONV_HALO, axis=0))
                    for x, halo in zip(xs, halos)]
        return _conv_silu(xs, shifteds, w_ref, colss)

    qs = [_l2norm(y) * (D ** -0.5) for y in conv(q_ref, hq_ref, wq_ref)]
    ks = [_l2norm(y) for y in conv(k_ref, hk_ref, wk_ref)]
    vs = conv(v_ref, hv_ref, wv_ref)
    cf = cf_ref[0]
    wus, q_effs, o0s, k_decs = _block_terms(qs, ks, vs, [cf] * hb, his, [gch_ref[0, hi] for hi in his],
                                            [bth_ref[0, hi] for hi in his], masks, n_fac)
    units = [(hi, u) for u in range(n_units) for hi in his]
    rowss = [slice(u * unit, (u + 1) * unit) for _, u in units]
    s_olds = [s0_ref[u, hi] for hi, u in units]
    xs = [_dot(jnp.concatenate([wus[hi][rows, :D], q_effs[hi][rows]], axis=0), s_old)
          for (hi, _), rows, s_old in zip(units, rowss, s_olds)]
    upds = [_dot_t0(k_decs[hi][rows], wus[hi][rows, D:] - x[:unit]) for (hi, _), rows, x in zip(units, rowss, xs)]
    for (hi, u), s_old, upd in zip(units, s_olds, upds):
        glast = cf[u * unit:u * unit + 1, CF_GLAST + hi:CF_GLAST + hi + 1]
        s_ref[u, hi] = s_old * jnp.exp(glast) + upd
    for hi in his:
        o = jnp.concatenate([o0s[hi][rows] + x[unit:] for (h2, _), rows, x in zip(units, rowss, xs) if h2 == hi],
                            axis=0)
        o_ref[:, colss[hi]] = _finish(o, z_ref[:, colss[hi]], ong).astype(o_ref.dtype)


def _delta_seq(proj, offs, conv_halo, w_conv8, gc_hm, beta_hm, cf, o_norm_g, *, hb, lt, unroll):
    T = proj.shape[0]
    B, H, N, _ = gc_hm.shape
    D = LANE
    W = hb * D
    L = T // B
    n_t = L // lt
    nt = lt // CHUNK
    hpt = lt // CONV_HALO

    def col_spec(off):
        return pl.BlockSpec((lt, W), lambda i, j, t, o=off // W: (i * n_t + t, o + j))

    def prev_spec(off):
        return pl.BlockSpec((CONV_HALO, W),
                            lambda i, j, t, o=off // W: (jnp.maximum((i * n_t + t) * hpt - 1, 0), o + j))

    def halo_spec(g):
        return pl.BlockSpec((1, CONV_HALO, W), lambda i, j, t, o=g * (H // hb): (i, 0, o + j))

    def w_spec(g):
        return pl.BlockSpec((CONV_HALO, W), lambda i, j, t, o=g * (H // hb): (0, o + j))

    hm_spec = pl.BlockSpec((1, hb, nt, CHUNK), lambda i, j, t: (i, j, t, 0))
    state_spec = pl.BlockSpec((1, hb, D, D), lambda i, j, t: (i, j, 0, 0))
    in_specs = ([col_spec(o) for o in offs] + [prev_spec(o) for o in offs[:3]] + [halo_spec(g) for g in range(3)]
                + [w_spec(g) for g in range(3)]
                + [hm_spec, hm_spec, pl.BlockSpec((1, lt, LANE), lambda i, j, t: (j, i * n_t + t, 0)),
                   pl.BlockSpec((1, D), lambda i, j, t: (0, 0))])
    args = [proj] * 7 + [conv_halo] * 3 + [w_conv8] * 3 + [gc_hm, beta_hm, cf, o_norm_g.reshape(1, D)]
    blocks = [4 * _nbytes((lt, W), F32), 9 * _nbytes((CONV_HALO, W), F32),
              2 * _nbytes((hb, max(nt, SUBLANE), LANE), F32), _nbytes((lt, LANE), F32),
              _nbytes((hb, D, D), F32), _nbytes((lt, W), BF16)]
    scratch = [pltpu.VMEM((hb, nt, D + CHUNK, D), F32), pltpu.VMEM((hb, nt, D, D), F32),
               pltpu.VMEM((hb, nt, CHUNK, D), F32)]
    return pl.pallas_call(
        functools.partial(_delta_seq_kernel, hb=hb, L=lt, unroll=unroll),
        grid=(B, H // hb, n_t),
        in_specs=in_specs,
        out_specs=[pl.BlockSpec((lt, W), lambda i, j, t: (i * n_t + t, j)), state_spec],
        out_shape=[jax.ShapeDtypeStruct((T, H * D), BF16), jax.ShapeDtypeStruct((B, H, D, D), F32)],
        scratch_shapes=scratch,
        compiler_params=_params(3, blocks, hb * nt * _nbytes((2 * D + 2 * CHUNK, D), F32)),
        name="delta_seq",
    )(*args)


def _delta_units(proj, offs, conv_halo, w_conv8, gc_hm, beta_hm, cf, s0, o_norm_g, *, unit, hb):
    T = proj.shape[0]
    nb, H, _, _ = gc_hm.shape
    D = LANE
    W = hb * D
    n_units = CHUNK // unit

    def col_spec(off):
        return pl.BlockSpec((CHUNK, W), lambda i, j, o=off // W: (i, o + j))

    def halo_spec(g):
        return pl.BlockSpec((CHUNK, W), lambda i, j, o=g * (H // hb): (i, o + j))

    def w_spec(g):
        return pl.BlockSpec((CONV_HALO, W), lambda i, j, o=g * (H // hb): (0, o + j))

    hm_spec = pl.BlockSpec((1, hb, 1, CHUNK), lambda i, j: (i, j, 0, 0))
    state_spec = pl.BlockSpec((n_units, hb, D, D), lambda i, j: (i, j, 0, 0))
    in_specs = ([col_spec(o) for o in offs] + [halo_spec(g) for g in range(3)] + [w_spec(g) for g in range(3)]
                + [hm_spec, hm_spec, pl.BlockSpec((1, CHUNK, LANE), lambda i, j: (j, i, 0)), state_spec,
                   pl.BlockSpec((1, D), lambda i, j: (0, 0))])
    args = [proj] * 4 + [conv_halo] * 3 + [w_conv8] * 3 + [gc_hm, beta_hm, cf, s0, o_norm_g.reshape(1, D)]
    blocks = [7 * _nbytes((CHUNK, W), F32), 3 * _nbytes((CONV_HALO, W), F32),
              2 * _nbytes((hb, SUBLANE, LANE), F32), _nbytes((CHUNK, LANE), F32),
              2 * _nbytes((n_units, hb, D, D), F32), _nbytes((CHUNK, W), BF16)]
    return pl.pallas_call(
        functools.partial(_delta_units_kernel, hb=hb, unit=unit),
        grid=(nb, H // hb),
        in_specs=in_specs,
        out_specs=[pl.BlockSpec((CHUNK, W), lambda i, j: (i, j)), state_spec],
        out_shape=[jax.ShapeDtypeStruct((T, H * D), BF16),
                   jax.ShapeDtypeStruct((nb * n_units, H, D, D), F32)],
        compiler_params=_params(2, blocks),
        name="delta_units",
    )(*args)


def _head_major(tok, nb, H, lane0):
    T = tok.shape[0]
    x = tok[:, lane0:lane0 + H].reshape(nb, T // nb // CHUNK, CHUNK, H)
    return jnp.transpose(x, (0, 3, 1, 2))


def _col_form(beta_tok, gc_tok, gl_tok, H, hb):
    T = beta_tok.shape[0]

    def part(tok, lane0, width):
        x = tok[:, lane0:lane0 + H].reshape(T, H // hb, hb)
        x = jnp.transpose(x, (1, 0, 2))
        return jnp.pad(x, ((0, 0), (0, 0), (0, width - hb)))

    return jnp.concatenate([part(beta_tok, 0, CF_GC - CF_BETA), part(gc_tok, H, CF_GLAST - CF_GC),
                            part(gl_tok, H, LANE - CF_GLAST)], axis=-1)


def _tail_rows(state, new, n):
    L = new.shape[1]
    if L >= n:
        return new[:, L - n:]
    return jnp.concatenate([state[:, state.shape[1] - (n - L):], new], axis=1)


def _layer(x, p, pool_state, conv_state, delta_state, start, lw, cfg):
    (w_in_t, w_tail_b, w_pool_b, pool_scale, w_conv8, alog_row, dtb_row, o_norm_g, w_out, ln1_g, ln1_b,
     w_gate_up, w_down_b, ln2_g, ln2_b, w_ple_gate, w_ple_proj_b, alpha) = lw
    B, L, D = x.shape
    T = B * L
    P, DN, H, d_ff = cfg["P"], cfg["DN"], cfg["H"], cfg["d_ff"]
    n_main = P + 4 * DN
    tm = _pick(T, (1024, 512, 256, 128))
    tm_big = _pick(T, (2048, 1024, 512, 256, 128))
    tm_ln = min(tm, 256)
    tm_down = min(tm, 512)
    tn_ff = _pick(d_ff, (256, 128))
    x2 = x.reshape(T, D)
    unit = math.gcd(L, CHUNK)
    xb, beta_tok, gc_tok, gl_tok = _gates(x2, w_tail_b, alog_row, dtb_row, unit=unit, tm=min(tm, 512))

    proj = _matmul_nt(xb, w_in_t, n_main, tm=tm_big, tn=_pick(n_main, (256, 128)), out_dtype=F32)
    proj3 = proj.reshape(B, L, n_main)
    n_pool, n_conv = pool_state.shape[1], conv_state.shape[1]
    new_pool = _tail_rows(pool_state, proj3[:, :, :P], n_pool)
    new_conv = _tail_rows(conv_state, proj3[:, :, P:P + 3 * DN], n_conv)

    pool_halo = jnp.pad(pool_state, ((0, 0), (POOL_HALO - n_pool, 0), (0, 0)))
    bb_pool = 1 if L >= 256 else _pick(B, (16, 8, 4, 2, 1))
    y_pool = _pool_mixer(proj, pool_halo, w_pool_b, pool_scale, bb=bb_pool, L=L, start=start)

    conv_halo = jnp.pad(conv_state, ((0, 0), (CONV_HALO - n_conv, 0), (0, 0)))
    offs = (P, P + DN, P + 2 * DN, P + 3 * DN)
    if unit == CHUNK:
        assert delta_state is None
        hb = _pick(H, (8, 4, 2, 1))
        o, s_new = _delta_seq(proj, offs, conv_halo, w_conv8, _head_major(gc_tok, B, H, H),
                              _head_major(beta_tok, B, H, 0), _col_form(beta_tok, gc_tok, gl_tok, H, hb),
                              o_norm_g, hb=hb, lt=_pick(L, (512, 256, 128, 64)), unroll=2)
    else:
        assert unit == CONV_HALO and T % CHUNK == 0
        nb, hb = T // CHUNK, _pick(H, (8, 4, 2, 1))
        o, s_new = _delta_units(proj, offs, conv_halo.reshape(B * CONV_HALO, 3 * DN), w_conv8,
                                _head_major(gc_tok, nb, H, H), _head_major(beta_tok, nb, H, 0),
                                _col_form(beta_tok, gc_tok, gl_tok, H, hb), delta_state, o_norm_g,
                                unit=unit, hb=hb)

    s1 = _matmul_resid((y_pool, o), w_out, x2, alpha, tm=tm_big, tn=tn_ff)
    h1b, h1_ln = _layer_norm(s1, ln1_g, ln1_b, tm=tm_ln)
    act = _swiglu(h1b, w_gate_up, d_ff, tm=tm, tn=tn_ff)
    s2 = _matmul_resid((act,), w_down_b, h1_ln, alpha, tm=tm_down, tn=_pick(D, (512, 256, 128)))
    h2b, h2_ln = _layer_norm(s2, ln2_g, ln2_b, tm=tm_ln)
    y = _ple(h2b, h2_ln, p.reshape(T, -1).astype(BF16), w_ple_gate, w_ple_proj_b, tm=tm_big, tn=tn_ff)
    return y.reshape(B, L, D), new_pool, new_conv, s_new


def kernel(x_prompt, x_sample, state_pool, state_conv, state_delta, p_prompt, p_sample, w_in, w_pool, pool_scale, w_conv, a_log, dt_bias, o_norm_g, w_out, ln1_g, ln1_b, w_gate_up, w_down, ln2_g, ln2_b, w_ple_gate, w_ple_proj):
    depth = w_in.shape[0]
    B = x_prompt.shape[0]
    D = x_prompt.shape[2]
    P = w_pool.shape[1] * w_pool.shape[2]
    H = a_log.shape[1]
    DN = H * LANE
    n_main = P + 4 * DN
    cfg = dict(P=P, DN=DN, H=H, d_ff=w_down.shape[1])
    alpha = (2.0 * depth) ** 0.25
    assert w_in.shape[2] == n_main + 2 * H and H <= CF_GC - CF_BETA and n_main % LANE == 0
    assert w_conv.shape[1] == 4 and D == P + DN and max(POOL_WINDOWS) - 1 == state_pool.shape[2]
    assert o_norm_g.shape[1] == LANE

    yp, ys = x_prompt, x_sample
    outs = [[] for _ in range(6)]
    for li in range(depth):
        w_in_t = jnp.swapaxes(w_in[li], 0, 1)
        lw = (w_in_t,
              jnp.pad(w_in_t[n_main:], ((0, LANE - 2 * H), (0, 0))).astype(BF16),
              w_pool[li].astype(BF16), pool_scale[li],
              jnp.pad(w_conv[li], ((0, CONV_HALO - w_conv.shape[1]), (0, 0))),
              jnp.pad(a_log[li], (H, LANE - 2 * H)).reshape(1, LANE),
              jnp.pad(dt_bias[li], (H, LANE - 2 * H)).reshape(1, LANE),
              o_norm_g[li], w_out[li], ln1_g[li], ln1_b[li], w_gate_up[li],
              w_down[li].astype(BF16), ln2_g[li], ln2_b[li], w_ple_gate[li],
              w_ple_proj[li].astype(BF16), alpha)
        zeros_pool = jnp.zeros((B,) + state_pool.shape[2:], F32)
        zeros_conv = jnp.zeros((B,) + state_conv.shape[2:], F32)
        yp, npl, ncv, nst = _layer(yp, p_prompt[li], zeros_pool, zeros_conv, None, 0, lw, cfg)
        ys, spl, scv, sst = _layer(ys, p_sample[li], state_pool[li], state_conv[li], state_delta[li], PAST_LEN,
                                   lw, cfg)
        for acc, val in zip(outs, (npl, ncv, nst, spl, scv, sst)):
            acc.append(val)
    return (yp, ys) + tuple(jnp.stack(o) for o in outs)
```

```python
import functools
import math

import jax
import jax.numpy as jnp
from jax import lax
from jax.experimental import pallas as pl
from jax.experimental.pallas import tpu as pltpu

F32 = jnp.float32
BF16 = jnp.bfloat16

PAST_LEN = 16384
POOL_WINDOWS = (2, 4, 8, 16)
POOL_HALO = 16
CONV_HALO = 8
CHUNK = 64
LN_EPS = 1e-5
RMS_EPS = 1e-6
L2_EPS = 1e-6

LANE = 128
SUBLANE = 8
VMEM_BUDGET = 56 * 1024 * 1024

CF_BETA, CF_GC, CF_GLAST = 0, 32, 64


def _vmem_limit(block_bytes, scratch_bytes=0):
    est = 2 * sum(block_bytes) + scratch_bytes + (12 << 20)
    return int(min(est, VMEM_BUDGET))


def _nbytes(shape, dtype):
    return math.prod(shape) * jnp.dtype(dtype).itemsize


def _pick(n, candidates):
    for c in candidates:
        if n % c == 0:
            return c
    raise ValueError(f"no tile for {n} among {candidates}")


def _params(n_axes, block_bytes, scratch_bytes=0):
    return pltpu.CompilerParams(dimension_semantics=("arbitrary",) * n_axes,
                                vmem_limit_bytes=_vmem_limit(block_bytes, scratch_bytes))


def _dot(a, b):
    return jnp.dot(a, b, preferred_element_type=F32)


def _dot_t0(a, b):
    return lax.dot_general(a, b, (((0,), (0,)), ((), ())), preferred_element_type=F32)


MM_SUB_ROWS = 1024


def _mm_nt_kernel(x_ref, wt_ref, o_ref):
    wt = wt_ref[...].astype(BF16)
    tm = x_ref.shape[0]
    sub = min(tm, MM_SUB_ROWS)
    for r in range(0, tm, sub):
        o_ref[r:r + sub, :] = lax.dot_general(x_ref[r:r + sub, :], wt, (((1,), (1,)), ((), ())),
                                              preferred_element_type=F32).astype(o_ref.dtype)


def _matmul_nt(x, wt, n_cols, *, tm, tn, out_dtype):
    T, K = x.shape
    blocks = [_nbytes((tm, K), x.dtype), _nbytes((tn, K), wt.dtype), _nbytes((tm, tn), out_dtype)]
    return pl.pallas_call(
        _mm_nt_kernel,
        grid=(T // tm, n_cols // tn),
        in_specs=[pl.BlockSpec((tm, K), lambda i, j: (i, 0)),
                  pl.BlockSpec((tn, K), lambda i, j: (j, 0))],
        out_specs=pl.BlockSpec((tm, tn), lambda i, j: (i, j)),
        out_shape=jax.ShapeDtypeStruct((T, n_cols), out_dtype),
        compiler_params=_params(2, blocks, _nbytes((tm, tn), F32) + _nbytes((tn, K), BF16)),
        name="matmul_nt",
    )(x, wt)


def _ln_apply(s, mu, rstd, g, b):
    reps = s.shape[1] // LANE
    mu = jnp.concatenate([mu] * reps, axis=1)
    rstd = jnp.concatenate([rstd] * reps, axis=1)
    return (s - mu) * rstd * g + b


def _ln_tile_specs(tm, tn):
    stat = pl.BlockSpec((tm, LANE), lambda i, j: (i, 0))
    vec = pl.BlockSpec((1, tn), lambda i, j: (0, j))
    return [pl.BlockSpec((tm, tn), lambda i, j: (i, j)), stat, stat, vec, vec]


def _mm_resid_kernel(*refs, alpha, splits, normed):
    x_refs, (w_ref, *r_refs, o_ref) = refs[:len(splits)], refs[len(splits):]
    w = w_ref[...].astype(BF16)
    tm = o_ref.shape[0]
    sub = min(tm, MM_SUB_ROWS)
    for r in range(0, tm, sub):
        rows = slice(r, r + sub)
        if normed:
            s_ref, mu_ref, rstd_ref, g_ref, b_ref = r_refs
            res = _ln_apply(s_ref[rows, :], mu_ref[rows, :], rstd_ref[rows, :], g_ref[...], b_ref[...])
        else:
            res = r_refs[0][rows, :]
        acc = alpha * res
        k0 = 0
        for x_ref, kw in zip(x_refs, splits):
            acc = acc + _dot(x_ref[rows, :], w[k0:k0 + kw, :])
            k0 += kw
        o_ref[rows, :] = acc


def _matmul_resid(xs, w, resid, alpha, *, tm, tn):
    T = xs[0].shape[0]
    splits = tuple(x.shape[1] for x in xs)
    K, N = w.shape
    normed = isinstance(resid, tuple)
    if normed:
        r_args, r_specs = list(resid), _ln_tile_specs(tm, tn)
    else:
        r_args, r_specs = [resid], [pl.BlockSpec((tm, tn), lambda i, j: (i, j))]
    blocks = [_nbytes((tm, K), BF16), _nbytes((K, tn), w.dtype), 2 * _nbytes((tm, tn), F32),
              2 * _nbytes((tm, LANE), F32)]
    return pl.pallas_call(
        functools.partial(_mm_resid_kernel, alpha=alpha, splits=splits, normed=normed),
        grid=(T // tm, N // tn),
        in_specs=[pl.BlockSpec((tm, kw), lambda i, j: (i, 0)) for kw in splits]
        + [pl.BlockSpec((K, tn), lambda i, j: (0, j))] + r_specs,
        out_specs=pl.BlockSpec((tm, tn), lambda i, j: (i, j)),
        out_shape=jax.ShapeDtypeStruct((T, N), F32),
        compiler_params=_params(2, blocks, _nbytes((tm, tn), F32)),
        name="matmul_resid",
    )(*xs, w, *r_args)


def _ln_kernel(s_ref, g_ref, b_ref, hb_ref, mu_ref, rstd_ref):
    s = s_ref[...]
    mu = jnp.mean(s, -1, keepdims=True)
    c = s - mu
    var = jnp.mean(c * c, -1, keepdims=True)
    rstd = lax.rsqrt(var + LN_EPS)
    hb_ref[...] = (c * rstd * g_ref[...] + b_ref[...]).astype(BF16)
    mu_ref[...] = jnp.broadcast_to(mu, mu_ref.shape)
    rstd_ref[...] = jnp.broadcast_to(rstd, rstd_ref.shape)


def _layer_norm(s, g, b, *, tm):
    T, D = s.shape
    blocks = [_nbytes((tm, D), F32), _nbytes((tm, D), BF16), 2 * _nbytes((1, D), F32), 2 * _nbytes((tm, LANE), F32)]
    row = pl.BlockSpec((tm, D), lambda i: (i, 0))
    vec = pl.BlockSpec((1, D), lambda i: (0, 0))
    stat = pl.BlockSpec((tm, LANE), lambda i: (i, 0))
    g2, b2 = g.reshape(1, D), b.reshape(1, D)
    hb, mu, rstd = pl.pallas_call(
        _ln_kernel,
        grid=(T // tm,),
        in_specs=[row, vec, vec],
        out_specs=[row, stat, stat],
        out_shape=[jax.ShapeDtypeStruct((T, D), BF16)] + [jax.ShapeDtypeStruct((T, LANE), F32)] * 2,
        compiler_params=_params(1, blocks, 2 * _nbytes((tm, D), F32)),
        name="layer_norm",
    )(s, g2, b2)
    return hb, (s, mu, rstd, g2, b2)


def _swiglu_kernel(h_ref, wg_ref, wu_ref, o_ref):
    h = h_ref[...]
    gate = _dot(h, wg_ref[...].astype(BF16))
    up = _dot(h, wu_ref[...].astype(BF16))
    o_ref[...] = (gate * jax.nn.sigmoid(gate) * up).astype(o_ref.dtype)


def _swiglu(h, w_gate_up, d_ff, *, tm, tn):
    T, K = h.shape
    n_j = d_ff // tn
    blocks = [_nbytes((tm, K), BF16), 2 * _nbytes((K, tn), w_gate_up.dtype), _nbytes((tm, tn), BF16)]
    return pl.pallas_call(
        _swiglu_kernel,
        grid=(T // tm, n_j),
        in_specs=[pl.BlockSpec((tm, K), lambda i, j: (i, 0)),
                  pl.BlockSpec((K, tn), lambda i, j: (0, j)),
                  pl.BlockSpec((K, tn), lambda i, j: (0, j + n_j))],
        out_specs=pl.BlockSpec((tm, tn), lambda i, j: (i, j)),
        out_shape=jax.ShapeDtypeStruct((T, d_ff), BF16),
        compiler_params=_params(2, blocks, 3 * _nbytes((tm, tn), F32)),
        name="swiglu",
    )(h, w_gate_up, w_gate_up)


def _ple_kernel(hb_ref, wg_ref, p_ref, wp_ref, s_ref, mu_ref, rstd_ref, g_ref, b_ref, y_ref):
    wg = wg_ref[...].astype(BF16)
    wp = wp_ref[...]
    tm = y_ref.shape[0]
    sub = min(tm, MM_SUB_ROWS)
    for r in range(0, tm, sub):
        rows = slice(r, r + sub)
        h = _ln_apply(s_ref[rows, :], mu_ref[rows, :], rstd_ref[rows, :], g_ref[...], b_ref[...])
        gate = jax.nn.sigmoid(_dot(hb_ref[rows, :], wg))
        y_ref[rows, :] = h + gate * _dot(p_ref[rows, :], wp)


def _ple(hb, h_ln, p, w_gate, w_proj, *, tm, tn):
    T, D = hb.shape
    P = p.shape[1]
    blocks = [_nbytes((tm, D), BF16), _nbytes((D, tn), w_gate.dtype), _nbytes((tm, P), BF16),
              _nbytes((P, tn), BF16), 2 * _nbytes((tm, tn), F32), 2 * _nbytes((tm, LANE), F32)]
    return pl.pallas_call(
        _ple_kernel,
        grid=(T // tm, D // tn),
        in_specs=[pl.BlockSpec((tm, D), lambda i, j: (i, 0)),
                  pl.BlockSpec((D, tn), lambda i, j: (0, j)),
                  pl.BlockSpec((tm, P), lambda i, j: (i, 0)),
                  pl.BlockSpec((P, tn), lambda i, j: (0, j))] + _ln_tile_specs(tm, tn),
        out_specs=pl.BlockSpec((tm, tn), lambda i, j: (i, j)),
        out_shape=jax.ShapeDtypeStruct((T, D), F32),
        compiler_params=_params(2, blocks, 3 * _nbytes((tm, tn), F32)),
        name="ple",
    )(hb, w_gate, p, w_proj, *h_ln)


def _pool_kernel(u_ref, halo_ref, w_ref, scale_ref, o_ref, d_ref, *, bb, L, tl, start):
    P = u_ref.shape[1]
    G = len(POOL_WINDOWS)
    gw = P // G
    n_sub = L // tl
    row = lax.broadcasted_iota(jnp.int32, (tl, 1), 0)

    def sub_tile(bi, t, tail):
        r0 = pl.multiple_of(bi * L + t * tl, SUBLANE)
        x = u_ref[pl.ds(r0, tl), :]
        ext = jnp.concatenate([tail, x], axis=0)
        pos = start + t * tl + row
        for gi, win in enumerate(POOL_WINDOWS):
            cols = slice(gi * gw, (gi + 1) * gw)
            s = ext[:, cols]
            shift = 1
            while shift < win:
                s = s + pltpu.roll(s, shift, axis=0)
                shift *= 2
            cnt = jnp.minimum(pos + 1, win).astype(F32)
            d = s[POOL_HALO:] / cnt - x[:, cols]
            d_ref[pl.ds(r0, tl), cols] = d
        return ext[tl:]

    def seq(bi, carry):
        tail0 = halo_ref[bi]
        if n_sub == 1:
            sub_tile(bi, 0, tail0)
        else:
            lax.fori_loop(0, n_sub, lambda t, tail: sub_tile(bi, t, tail), tail0)
        return carry

    lax.fori_loop(0, bb, seq, 0)
    for gi in range(G):
        cols = slice(gi * gw, (gi + 1) * gw)
        y = _dot(d_ref[:, cols].astype(BF16), w_ref[gi])
        o_ref[:, cols] = (y * scale_ref[:, cols]).astype(o_ref.dtype)


def _pool_mixer(proj, halo, w_pool, pool_scale, *, bb, L, start):
    B, _, P = halo.shape
    G, gw, _ = w_pool.shape
    rows = bb * L
    tl = min(L, 256)
    blocks = [_nbytes((rows, P), F32), _nbytes((bb, POOL_HALO, P), F32), _nbytes(w_pool.shape, BF16),
              _nbytes((1, P), F32), _nbytes((rows, P), BF16)]
    return pl.pallas_call(
        functools.partial(_pool_kernel, bb=bb, L=L, tl=tl, start=start),
        grid=(B // bb,),
        in_specs=[pl.BlockSpec((rows, P), lambda i: (i, 0)),
                  pl.BlockSpec((bb, POOL_HALO, P), lambda i: (i, 0, 0)),
                  pl.BlockSpec((G, gw, gw), lambda i: (0, 0, 0)),
                  pl.BlockSpec((1, P), lambda i: (0, 0))],
        out_specs=pl.BlockSpec((rows, P), lambda i: (i, 0)),
        out_shape=jax.ShapeDtypeStruct((B * L, P), BF16),
        scratch_shapes=[pltpu.VMEM((rows, P), F32)],
        compiler_params=_params(1, blocks, _nbytes((rows, P), F32) + 4 * _nbytes((tl + POOL_HALO, P), F32)),
        name="pool_mixer",
    )(proj, halo, w_pool, pool_scale.reshape(1, P))


def _gates_kernel(x_ref, w_ref, alog_ref, dtb_ref, xb_ref, beta_ref, gc_ref, gl_ref, *, unit):
    xb = x_ref[...].astype(BF16)
    xb_ref[...] = xb
    t = lax.dot_general(xb, w_ref[...], (((1,), (1,)), ((), ())),
                        preferred_element_type=F32)
    beta_ref[...] = jax.nn.sigmoid(t)
    x = t + dtb_ref[...]
    softplus = jnp.maximum(x, 0.0) + jnp.log1p(jnp.exp(-jnp.abs(x)))
    g = -jnp.exp(alog_ref[...]) * softplus
    rows = g.shape[0]
    pos = lax.broadcasted_iota(jnp.int32, g.shape, 0) % unit
    shift = 1
    while shift < unit:
        g = g + jnp.where(pos >= shift, pltpu.roll(g, shift, axis=0), 0.0)
        shift *= 2
    gc_ref[...] = g
    shift = 1
    while shift < unit:
        g = jnp.where((pos & shift) == 0, pltpu.roll(g, rows - shift, axis=0), g)
        shift *= 2
    gl_ref[...] = g


def _gates(x, w_tail, alog_row, dtb_row, *, unit, tm):
    T, D = x.shape
    spec = pl.BlockSpec((tm, LANE), lambda i: (i, 0))
    vec = pl.BlockSpec((1, LANE), lambda i: (0, 0))
    row = pl.BlockSpec((tm, D), lambda i: (i, 0))
    blocks = [_nbytes((tm, D), F32), _nbytes((tm, D), BF16), _nbytes((LANE, D), BF16), 3 * _nbytes((tm, LANE), F32)]
    return pl.pallas_call(
        functools.partial(_gates_kernel, unit=unit),
        grid=(T // tm,),
        in_specs=[row, pl.BlockSpec((LANE, D), lambda i: (0, 0)), vec, vec],
        out_specs=[row, spec, spec, spec],
        out_shape=[jax.ShapeDtypeStruct((T, D), BF16)] + [jax.ShapeDtypeStruct((T, LANE), F32)] * 3,
        compiler_params=_params(1, blocks, 8 * _nbytes((tm, LANE), F32)),
        name="gates",
    )(x, w_tail, alog_row, dtb_row)


def _zmap(fn, *lists):
    return [fn(*args) for args in zip(*lists)]


def _block_terms(qs, ks, vs, cfs, his, grows, brows, masks, n_fac):
    incl, strict, eye = masks
    C, D = qs[0].shape
    bcols = [cf[:, CF_BETA + hi:CF_BETA + hi + 1] for cf, hi in zip(cfs, his)]
    gcols = [cf[:, CF_GC + hi:CF_GC + hi + 1] for cf, hi in zip(cfs, his)]
    glcols = [cf[:, CF_GLAST + hi:CF_GLAST + hi + 1] for cf, hi in zip(cfs, his)]
    decays = _zmap(lambda gc, gr: jnp.where(incl, jnp.exp(jnp.where(incl, gc - gr, 0.0)), 0.0), gcols, grows)
    qk2s = _zmap(lambda q, k: lax.dot_general(jnp.concatenate([q, k], axis=0), k, (((1,), (1,)), ((), ())),
                                              preferred_element_type=F32), qs, ks)
    a_s = _zmap(lambda qk2, dc, bc: jnp.where(strict, qk2[C:] * dc * bc, 0.0), qk2s, decays, bcols)
    ps = [jnp.where(eye, 1.0, 0.0) - a for a in a_s]
    if n_fac > 0:
        sqs = [_dot(a, a) for a in a_s]
        for f in range(n_fac):
            if f < n_fac - 1:
                rs = _zmap(lambda p, sq: _dot(jnp.concatenate([p, sq], axis=0), sq), ps, sqs)
                ps = _zmap(lambda p, r: p + r[:C], ps, rs)
                sqs = [r[C:] for r in rs]
            else:
                ps = _zmap(lambda p, sq: p + _dot(p, sq), ps, sqs)
    tbs = _zmap(lambda p, br: p * br, ps, brows)
    w1s = _zmap(lambda tb, gr, k: _dot(tb * jnp.exp(gr), k), tbs, grows, ks)
    u1s = _zmap(_dot, tbs, vs)
    wus = _zmap(lambda w1, u1: jnp.concatenate([w1, u1], axis=1), w1s, u1s)
    tops = _zmap(lambda qk2, dc, wu: _dot(jnp.where(incl, qk2[:C] * dc, 0.0), wu), qk2s, decays, wus)
    q_effs = _zmap(lambda q, gc, top: q * jnp.exp(gc) - top[:, :D], qs, gcols, tops)
    k_decs = _zmap(lambda k, gl, gc: k * jnp.exp(gl - gc), ks, glcols, gcols)
    return wus, q_effs, [top[:, D:] for top in tops], k_decs


def _conv_silu(xs, shifteds, w_ref, colss):
    ys = _zmap(lambda x, cols: x * w_ref[3:4, cols], xs, colss)
    for s in (1, 2, 3):
        ys = _zmap(lambda y, sh, cols: y + sh(s) * w_ref[3 - s:4 - s, cols], ys, shifteds, colss)
    return [y * jax.nn.sigmoid(y) for y in ys]


def _l2norm(x, scale=1.0):
    return x * (lax.rsqrt(jnp.sum(x * x, -1, keepdims=True) + L2_EPS) * scale)


def _finish(o, z, ong):
    o = o * lax.rsqrt(jnp.mean(o * o, -1, keepdims=True) + RMS_EPS) * ong
    return o * (z * jax.nn.sigmoid(z))


def _masks(C, unit):
    ri = lax.broadcasted_iota(jnp.int32, (C, C), 0)
    ci = lax.broadcasted_iota(jnp.int32, (C, C), 1)
    same = (ri // unit) == (ci // unit)
    return same & (ri >= ci), same & (ri > ci), ri == ci


def _delta_seq_kernel(q_ref, k_ref, v_ref, z_ref, pq_ref, pk_ref, pv_ref, hq_ref, hk_ref, hv_ref,
                      wq_ref, wk_ref, wv_ref, gch_ref, bth_ref, cf_ref, ong_ref, o_ref, s_ref,
                      lhs_scr, n_scr, o0_scr, *, hb, L, unroll):
    C, D = CHUNK, LANE
    N = L // C
    masks = _masks(C, C)
    n_fac = int(math.log2(C)) - 1
    ong = ong_ref[...]
    first_tile = pl.program_id(2) == 0

    @pl.when(first_tile)
    def _():
        for hi in range(hb):
            s_ref[0, hi] = jnp.zeros((D, D), F32)

    def group(g):
        return [(g * unroll + u, hi) for u in range(unroll) for hi in range(hb)]

    def prep(g):
        items = group(g)
        colss = [slice(hi * D, (hi + 1) * D) for _, hi in items]

        def conv(x_ref, p_ref, h_ref, w_ref):
            xs = [x_ref[n * C:(n + 1) * C, cols] for (n, _), cols in zip(items, colss)]
            prevs = [x_ref[n * C - CONV_HALO:n * C, cols] if n > 0
                     else jnp.where(first_tile, h_ref[0, :, cols], p_ref[:, cols])
                     for (n, _), cols in zip(items, colss)]
            exts = _zmap(lambda prev, x: jnp.concatenate([prev, x], axis=0), prevs, xs)
            shifteds = [lambda s, ext=ext: pltpu.roll(ext, s, axis=0)[CONV_HALO:] for ext in exts]
            return _conv_silu(xs, shifteds, w_ref, colss)

        qs = [_l2norm(y, D ** -0.5) for y in conv(q_ref, pq_ref, hq_ref, wq_ref)]
        ks = [_l2norm(y) for y in conv(k_ref, pk_ref, hk_ref, wk_ref)]
        return qs, ks, conv(v_ref, pv_ref, hv_ref, wv_ref)

    def terms(g, qkv):
        items = group(g)
        his = [hi for _, hi in items]
        cfs = [cf_ref[0, n * C:(n + 1) * C, :] for n, _ in items]
        grows = [gch_ref[0, hi, n:n + 1, :] for n, hi in items]
        brows = [bth_ref[0, hi, n:n + 1, :] for n, hi in items]
        wus, q_effs, o0s, k_decs = _block_terms(*qkv, cfs, his, grows, brows, masks, n_fac)
        mns = _zmap(_dot_t0, k_decs, wus)
        for (n, hi), mn, q_eff, o0 in zip(items, mns, q_effs, o0s):
            lhs_scr[hi, n, :D, :] = mn[:, :D]
            lhs_scr[hi, n, D:, :] = q_eff
            n_scr[hi, n] = mn[:, D:]
            o0_scr[hi, n] = o0

    n_groups = N // unroll
    qkv = prep(0)
    for g in range(n_groups):
        nxt = prep(g + 1) if g + 1 < n_groups else None
        terms(g, qkv)
        qkv = nxt

    heads = list(range(hb))
    states = [s_ref[0, hi] for hi in heads]
    for n in range(N):
        rs = [_dot(lhs_scr[hi, n], s) for hi, s in zip(heads, states)]
        states = [s * jnp.exp(gch_ref[0, hi, n:n + 1, :][:, C - 1:C]) - r[:D] + n_scr[hi, n]
                  for hi, s, r in zip(heads, states, rs)]
        for hi, r in zip(heads, rs):
            cols = slice(hi * D, (hi + 1) * D)
            o = o0_scr[hi, n] + r[D:]
            o_ref[n * C:(n + 1) * C, cols] = _finish(o, z_ref[n * C:(n + 1) * C, cols], ong).astype(o_ref.dtype)
    for hi, s in zip(heads, states):
        s_ref[0, hi] = s


def _delta_units_kernel(q_ref, k_ref, v_ref, z_ref, hq_ref, hk_ref, hv_ref, wq_ref, wk_ref, wv_ref,
                        gch_ref, bth_ref, cf_ref, s0_ref, ong_ref, o_ref, s_ref, *, hb, unit):
    C, D = CHUNK, LANE
    n_units = C // unit
    masks = _masks(C, unit)
    n_fac = max(int(math.log2(unit)) - 1, 0)
    ong = ong_ref[...]
    pos = lax.broadcasted_iota(jnp.int32, (C, 1), 0) % unit
    his = list(range(hb))
    colss = [slice(hi * D, (hi + 1) * D) for hi in his]

    def conv(x_ref, h_ref, w_ref):
        xs = [x_ref[:, cols] for cols in colss]
        halos = [h_ref[:, cols] for cols in colss]
        shifteds = [lambda s, x=x, halo=halo: jnp.where(pos >= s, pltpu.roll(x, s, axis=0),
                                                        pltpu.roll(halo, C + s - CONV_HALO, axis=0))
                    for x, halo in zip(xs, halos)]
        return _conv_silu(xs, shifteds, w_ref, colss)

    qs = [_l2norm(y, D ** -0.5) for y in conv(q_ref, hq_ref, wq_ref)]
    ks = [_l2norm(y) for y in conv(k_ref, hk_ref, wk_ref)]
    vs = conv(v_ref, hv_ref, wv_ref)
    cf = cf_ref[0]
    wus, q_effs, o0s, k_decs = _block_terms(qs, ks, vs, [cf] * hb, his, [gch_ref[0, hi] for hi in his],
                                            [bth_ref[0, hi] for hi in his], masks, n_fac)
    units = [(hi, u) for u in range(n_units) for hi in his]
    rowss = [slice(u * unit, (u + 1) * unit) for _, u in units]
    s_olds = [s0_ref[u, hi] for hi, u in units]
    xs = [_dot(jnp.concatenate([wus[hi][rows, :D], q_effs[hi][rows]], axis=0), s_old)
          for (hi, _), rows, s_old in zip(units, rowss, s_olds)]
    upds = [_dot_t0(k_decs[hi][rows], wus[hi][rows, D:] - x[:unit]) for (hi, _), rows, x in zip(units, rowss, xs)]
    for (hi, u), s_old, upd in zip(units, s_olds, upds):
        glast = cf[u * unit:u * unit + 1, CF_GLAST + hi:CF_GLAST + hi + 1]
        s_ref[u, hi] = s_old * jnp.exp(glast) + upd
    for hi in his:
        o = jnp.concatenate([o0s[hi][rows] + x[unit:] for (h2, _), rows, x in zip(units, rowss, xs) if h2 == hi],
                            axis=0)
        o_ref[:, colss[hi]] = _finish(o, z_ref[:, colss[hi]], ong).astype(o_ref.dtype)


def _delta_seq(proj, offs, conv_halo, w_conv8, gc_hm, beta_hm, cf, o_norm_g, *, hb, lt, unroll):
    T = proj.shape[0]
    B, H, N, _ = gc_hm.shape
    D = LANE
    W = hb * D
    L = T // B
    n_t = L // lt
    nt = lt // CHUNK
    hpt = lt // CONV_HALO

    def col_spec(off):
        return pl.BlockSpec((lt, W), lambda i, j, t, o=off // W: (i * n_t + t, o + j))

    def prev_spec(off):
        return pl.BlockSpec((CONV_HALO, W),
                            lambda i, j, t, o=off // W: (jnp.maximum((i * n_t + t) * hpt - 1, 0), o + j))

    def halo_spec(g):
        return pl.BlockSpec((1, CONV_HALO, W), lambda i, j, t, o=g * (H // hb): (i, 0, o + j))

    def w_spec(g):
        return pl.BlockSpec((CONV_HALO, W), lambda i, j, t, o=g * (H // hb): (0, o + j))

    hm_spec = pl.BlockSpec((1, hb, nt, CHUNK), lambda i, j, t: (i, j, t, 0))
    state_spec = pl.BlockSpec((1, hb, D, D), lambda i, j, t: (i, j, 0, 0))
    in_specs = ([col_spec(o) for o in offs] + [prev_spec(o) for o in offs[:3]] + [halo_spec(g) for g in range(3)]
                + [w_spec(g) for g in range(3)]
                + [hm_spec, hm_spec, pl.BlockSpec((1, lt, LANE), lambda i, j, t: (j, i * n_t + t, 0)),
                   pl.BlockSpec((1, D), lambda i, j, t: (0, 0))])
    args = [proj] * 7 + [conv_halo] * 3 + [w_conv8] * 3 + [gc_hm, beta_hm, cf, o_norm_g.reshape(1, D)]
    blocks = [4 * _nbytes((lt, W), F32), 9 * _nbytes((CONV_HALO, W), F32),
              2 * _nbytes((hb, max(nt, SUBLANE), LANE), F32), _nbytes((lt, LANE), F32),
              _nbytes((hb, D, D), F32), _nbytes((lt, W), BF16)]
    scratch = [pltpu.VMEM((hb, nt, D + CHUNK, D), F32), pltpu.VMEM((hb, nt, D, D), F32),
               pltpu.VMEM((hb, nt, CHUNK, D), F32)]
    return pl.pallas_call(
        functools.partial(_delta_seq_kernel, hb=hb, L=lt, unroll=unroll),
        grid=(B, H // hb, n_t),
        in_specs=in_specs,
        out_specs=[pl.BlockSpec((lt, W), lambda i, j, t: (i * n_t + t, j)), state_spec],
        out_shape=[jax.ShapeDtypeStruct((T, H * D), BF16), jax.ShapeDtypeStruct((B, H, D, D), F32)],
        scratch_shapes=scratch,
        compiler_params=_params(3, blocks, hb * nt * _nbytes((2 * D + 2 * CHUNK, D), F32)),
        name="delta_seq",
    )(*args)


def _delta_units(proj, offs, conv_halo, w_conv8, gc_hm, beta_hm, cf, s0, o_norm_g, *, unit, hb):
    T = proj.shape[0]
    nb, H, _, _ = gc_hm.shape
    D = LANE
    W = hb * D
    n_units = CHUNK // unit

    def col_spec(off):
        return pl.BlockSpec((CHUNK, W), lambda i, j, o=off // W: (i, o + j))

    def halo_spec(g):
        return pl.BlockSpec((CHUNK, W), lambda i, j, o=g * (H // hb): (i, o + j))

    def w_spec(g):
        return pl.BlockSpec((CONV_HALO, W), lambda i, j, o=g * (H // hb): (0, o + j))

    hm_spec = pl.BlockSpec((1, hb, 1, CHUNK), lambda i, j: (i, j, 0, 0))
    state_spec = pl.BlockSpec((n_units, hb, D, D), lambda i, j: (i, j, 0, 0))
    in_specs = ([col_spec(o) for o in offs] + [halo_spec(g) for g in range(3)] + [w_spec(g) for g in range(3)]
                + [hm_spec, hm_spec, pl.BlockSpec((1, CHUNK, LANE), lambda i, j: (j, i, 0)), state_spec,
                   pl.BlockSpec((1, D), lambda i, j: (0, 0))])
    args = [proj] * 4 + [conv_halo] * 3 + [w_conv8] * 3 + [gc_hm, beta_hm, cf, s0, o_norm_g.reshape(1, D)]
    blocks = [7 * _nbytes((CHUNK, W), F32), 3 * _nbytes((CONV_HALO, W), F32),
              2 * _nbytes((hb, SUBLANE, LANE), F32), _nbytes((CHUNK, LANE), F32),
              2 * _nbytes((n_units, hb, D, D), F32), _nbytes((CHUNK, W), BF16)]
    return pl.pallas_call(
        functools.partial(_delta_units_kernel, hb=hb, unit=unit),
        grid=(nb, H // hb),
        in_specs=in_specs,
        out_specs=[pl.BlockSpec((CHUNK, W), lambda i, j: (i, j)), state_spec],
        out_shape=[jax.ShapeDtypeStruct((T, H * D), BF16),
                   jax.ShapeDtypeStruct((nb * n_units, H, D, D), F32)],
        compiler_params=_params(2, blocks),
        name="delta_units",
    )(*args)


def _head_major(tok, nb, H, lane0):
    T = tok.shape[0]
    x = tok[:, lane0:lane0 + H].reshape(nb, T // nb // CHUNK, CHUNK, H)
    return jnp.transpose(x, (0, 3, 1, 2))


def _col_form(beta_tok, gc_tok, gl_tok, H, hb):
    T = beta_tok.shape[0]

    def part(tok, lane0, width):
        x = tok[:, lane0:lane0 + H].reshape(T, H // hb, hb)
        x = jnp.transpose(x, (1, 0, 2))
        return jnp.pad(x, ((0, 0), (0, 0), (0, width - hb)))

    return jnp.concatenate([part(beta_tok, 0, CF_GC - CF_BETA), part(gc_tok, H, CF_GLAST - CF_GC),
                            part(gl_tok, H, LANE - CF_GLAST)], axis=-1)


def _tail_rows(state, new, n):
    L = new.shape[1]
    if L >= n:
        return new[:, L - n:]
    return jnp.concatenate([state[:, state.shape[1] - (n - L):], new], axis=1)


def _layer(x, p, pool_state, conv_state, delta_state, start, lw, cfg):
    (w_in_t, w_tail_b, w_pool_b, pool_scale, w_conv8, alog_row, dtb_row, o_norm_g, w_out, ln1_g, ln1_b,
     w_gate_up, w_down_b, ln2_g, ln2_b, w_ple_gate, w_ple_proj_b, alpha) = lw
    B, L, D = x.shape
    T = B * L
    P, DN, H, d_ff = cfg["P"], cfg["DN"], cfg["H"], cfg["d_ff"]
    n_main = P + 4 * DN
    tm = _pick(T, (1024, 512, 256, 128))
    tm_big = _pick(T, (2048, 1024, 512, 256, 128))
    tm_ln = min(tm, 256)
    tm_down = min(tm, 512)
    tn_ff = _pick(d_ff, (256, 128))
    x2 = x.reshape(T, D)
    unit = math.gcd(L, CHUNK)
    xb, beta_tok, gc_tok, gl_tok = _gates(x2, w_tail_b, alog_row, dtb_row, unit=unit, tm=min(tm, 512))

    proj = _matmul_nt(xb, w_in_t, n_main, tm=tm_big, tn=_pick(n_main, (256, 128)), out_dtype=F32)
    proj3 = proj.reshape(B, L, n_main)
    n_pool, n_conv = pool_state.shape[1], conv_state.shape[1]
    new_pool = _tail_rows(pool_state, proj3[:, :, :P], n_pool)
    new_conv = _tail_rows(conv_state, proj3[:, :, P:P + 3 * DN], n_conv)

    pool_halo = jnp.pad(pool_state, ((0, 0), (POOL_HALO - n_pool, 0), (0, 0)))
    bb_pool = 1 if L >= 256 else _pick(B, (16, 8, 4, 2, 1))
    y_pool = _pool_mixer(proj, pool_halo, w_pool_b, pool_scale, bb=bb_pool, L=L, start=start)

    conv_halo = jnp.pad(conv_state, ((0, 0), (CONV_HALO - n_conv, 0), (0, 0)))
    offs = (P, P + DN, P + 2 * DN, P + 3 * DN)
    if unit == CHUNK:
        assert delta_state is None
        hb = _pick(H, (8, 4, 2, 1))
        o, s_new = _delta_seq(proj, offs, conv_halo, w_conv8, _head_major(gc_tok, B, H, H),
                              _head_major(beta_tok, B, H, 0), _col_form(beta_tok, gc_tok, gl_tok, H, hb),
                              o_norm_g, hb=hb, lt=_pick(L, (512, 256, 128, 64)), unroll=2)
    else:
        assert unit == CONV_HALO and T % CHUNK == 0
        nb, hb = T // CHUNK, _pick(H, (8, 4, 2, 1))
        o, s_new = _delta_units(proj, offs, conv_halo.reshape(B * CONV_HALO, 3 * DN), w_conv8,
                                _head_major(gc_tok, nb, H, H), _head_major(beta_tok, nb, H, 0),
                                _col_form(beta_tok, gc_tok, gl_tok, H, hb), delta_state, o_norm_g,
                                unit=unit, hb=hb)

    s1 = _matmul_resid((y_pool, o), w_out, x2, alpha, tm=tm_big, tn=tn_ff)
    h1b, h1_ln = _layer_norm(s1, ln1_g, ln1_b, tm=tm_ln)
    act = _swiglu(h1b, w_gate_up, d_ff, tm=tm, tn=tn_ff)
    s2 = _matmul_resid((act,), w_down_b, h1_ln, alpha, tm=tm_down, tn=_pick(D, (512, 256, 128)))
    h2b, h2_ln = _layer_norm(s2, ln2_g, ln2_b, tm=tm_ln)
    y = _ple(h2b, h2_ln, p.reshape(T, -1).astype(BF16), w_ple_gate, w_ple_proj_b, tm=tm_big, tn=tn_ff)
    return y.reshape(B, L, D), new_pool, new_conv, s_new


def kernel(x_prompt, x_sample, state_pool, state_conv, state_delta, p_prompt, p_sample, w_in, w_pool, pool_scale, w_conv, a_log, dt_bias, o_norm_g, w_out, ln1_g, ln1_b, w_gate_up, w_down, ln2_g, ln2_b, w_ple_gate, w_ple_proj):
    depth = w_in.shape[0]
    B = x_prompt.shape[0]
    D = x_prompt.shape[2]
    P = w_pool.shape[1] * w_pool.shape[2]
    H = a_log.shape[1]
    DN = H * LANE
    n_main = P + 4 * DN
    cfg = dict(P=P, DN=DN, H=H, d_ff=w_down.shape[1])
    alpha = (2.0 * depth) ** 0.25
    assert w_in.shape[2] == n_main + 2 * H and H <= CF_GC - CF_BETA and n_main % LANE == 0
    assert w_conv.shape[1] == 4 and D == P + DN and max(POOL_WINDOWS) - 1 == state_pool.shape[2]
    assert o_norm_g.shape[1] == LANE

    yp, ys = x_prompt, x_sample
    outs = [[] for _ in range(6)]
    for li in range(depth):
        w_in_t = jnp.swapaxes(w_in[li], 0, 1)
        lw = (w_in_t,
              jnp.pad(w_in_t[n_main:], ((0, LANE - 2 * H), (0, 0))).astype(BF16),
              w_pool[li].astype(BF16), pool_scale[li],
              jnp.pad(w_conv[li], ((0, CONV_HALO - w_conv.shape[1]), (0, 0))),
              jnp.pad(a_log[li], (H, LANE - 2 * H)).reshape(1, LANE),
              jnp.pad(dt_bias[li], (H, LANE - 2 * H)).reshape(1, LANE),
              o_norm_g[li], w_out[li], ln1_g[li], ln1_b[li], w_gate_up[li],
              w_down[li].astype(BF16), ln2_g[li], ln2_b[li], w_ple_gate[li],
              w_ple_proj[li].astype(BF16), alpha)
        zeros_pool = jnp.zeros((B,) + state_pool.shape[2:], F32)
        zeros_conv = jnp.zeros((B,) + state_conv.shape[2:], F32)
        yp, npl, ncv, nst = _layer(yp, p_prompt[li], zeros_pool, zeros_conv, None, 0, lw, cfg)
        ys, spl, scv, sst = _layer(ys, p_sample[li], state_pool[li], state_conv[li], state_delta[li], PAST_LEN,
                                   lw, cfg)
        for acc, val in zip(outs, (npl, ncv, nst, spl, scv, sst)):
            acc.append(val)
    return (yp, ys) + tuple(jnp.stack(o) for o in outs)
```

```python
import functools
import math

import jax
import jax.numpy as jnp
from jax import lax
from jax.experimental import pallas as pl
from jax.experimental.pallas import tpu as pltpu

F32 = jnp.float32
BF16 = jnp.bfloat16

PAST_LEN = 16384
POOL_WINDOWS = (2, 4, 8, 16)
POOL_HALO = 16
CONV_HALO = 8
CHUNK = 64
LN_EPS = 1e-5
RMS_EPS = 1e-6
L2_EPS = 1e-6

LANE = 128
SUBLANE = 8
VMEM_BUDGET = 56 * 1024 * 1024

CF_BETA, CF_GC, CF_GLAST = 0, 32, 64


def _vmem_limit(block_bytes, scratch_bytes=0):
    est = 2 * sum(block_bytes) + scratch_bytes + (12 << 20)
    return int(min(est, VMEM_BUDGET))


def _nbytes(shape, dtype):
    return math.prod(shape) * jnp.dtype(dtype).itemsize


def _pick(n, candidates):
    for c in candidates:
        if n % c == 0:
            return c
    raise ValueError(f"no tile for {n} among {candidates}")


def _params(n_axes, block_bytes, scratch_bytes=0):
    return pltpu.CompilerParams(dimension_semantics=("arbitrary",) * n_axes,
                                vmem_limit_bytes=_vmem_limit(block_bytes, scratch_bytes))


def _dot(a, b):
    return jnp.dot(a, b, preferred_element_type=F32)


def _dot_t0(a, b):
    return lax.dot_general(a, b, (((0,), (0,)), ((), ())), preferred_element_type=F32)


MM_SUB_ROWS = 1024


def _mm_nt_kernel(x_ref, wt_ref, o_ref):
    wt = wt_ref[...].astype(BF16)
    tm = x_ref.shape[0]
    sub = min(tm, MM_SUB_ROWS)
    for r in range(0, tm, sub):
        o_ref[r:r + sub, :] = lax.dot_general(x_ref[r:r + sub, :], wt, (((1,), (1,)), ((), ())),
                                              preferred_element_type=F32).astype(o_ref.dtype)


def _matmul_nt(x, wt, n_cols, *, tm, tn, out_dtype):
    T, K = x.shape
    blocks = [_nbytes((tm, K), x.dtype), _nbytes((tn, K), wt.dtype), _nbytes((tm, tn), out_dtype)]
    return pl.pallas_call(
        _mm_nt_kernel,
        grid=(T // tm, n_cols // tn),
        in_specs=[pl.BlockSpec((tm, K), lambda i, j: (i, 0)),
                  pl.BlockSpec((tn, K), lambda i, j: (j, 0))],
        out_specs=pl.BlockSpec((tm, tn), lambda i, j: (i, j)),
        out_shape=jax.ShapeDtypeStruct((T, n_cols), out_dtype),
        compiler_params=_params(2, blocks, _nbytes((tm, tn), F32) + _nbytes((tn, K), BF16)),
        name="matmul_nt",
    )(x, wt)


def _ln_apply(s, mu, rstd, g, b):
    reps = s.shape[1] // LANE
    mu = jnp.concatenate([mu] * reps, axis=1)
    rstd = jnp.concatenate([rstd] * reps, axis=1)
    return (s - mu) * rstd * g + b


def _ln_tile_specs(tm, tn):
    stat = pl.BlockSpec((tm, LANE), lambda i, j: (i, 0))
    vec = pl.BlockSpec((1, tn), lambda i, j: (0, j))
    return [pl.BlockSpec((tm, tn), lambda i, j: (i, j)), stat, stat, vec, vec]


def _mm_resid_kernel(*refs, alpha, splits, normed):
    x_refs, (w_ref, *r_refs, o_ref) = refs[:len(splits)], refs[len(splits):]
    w = w_ref[...].astype(BF16)
    tm = o_ref.shape[0]
    sub = min(tm, MM_SUB_ROWS)
    for r in range(0, tm, sub):
        rows = slice(r, r + sub)
        if normed:
            s_ref, mu_ref, rstd_ref, g_ref, b_ref = r_refs
            res = _ln_apply(s_ref[rows, :], mu_ref[rows, :], rstd_ref[rows, :], g_ref[...], b_ref[...])
        else:
            res = r_refs[0][rows, :]
        acc = alpha * res
        k0 = 0
        for x_ref, kw in zip(x_refs, splits):
            acc = acc + _dot(x_ref[rows, :], w[k0:k0 + kw, :])
            k0 += kw
        o_ref[rows, :] = acc


def _matmul_resid(xs, w, resid, alpha, *, tm, tn):
    T = xs[0].shape[0]
    splits = tuple(x.shape[1] for x in xs)
    K, N = w.shape
    normed = isinstance(resid, tuple)
    if normed:
        r_args, r_specs = list(resid), _ln_tile_specs(tm, tn)
    else:
        r_args, r_specs = [resid], [pl.BlockSpec((tm, tn), lambda i, j: (i, j))]
    blocks = [_nbytes((tm, K), BF16), _nbytes((K, tn), w.dtype), 2 * _nbytes((tm, tn), F32),
              2 * _nbytes((tm, LANE), F32)]
    return pl.pallas_call(
        functools.partial(_mm_resid_kernel, alpha=alpha, splits=splits, normed=normed),
        grid=(T // tm, N // tn),
        in_specs=[pl.BlockSpec((tm, kw), lambda i, j: (i, 0)) for kw in splits]
        + [pl.BlockSpec((K, tn), lambda i, j: (0, j))] + r_specs,
        out_specs=pl.BlockSpec((tm, tn), lambda i, j: (i, j)),
        out_shape=jax.ShapeDtypeStruct((T, N), F32),
        compiler_params=_params(2, blocks, _nbytes((tm, tn), F32)),
        name="matmul_resid",
    )(*xs, w, *r_args)


def _ln_kernel(s_ref, g_ref, b_ref, hb_ref, mu_ref, rstd_ref):
    s = s_ref[...]
    mu = jnp.mean(s, -1, keepdims=True)
    c = s - mu
    var = jnp.mean(c * c, -1, keepdims=True)
    rstd = lax.rsqrt(var + LN_EPS)
    hb_ref[...] = (c * rstd * g_ref[...] + b_ref[...]).astype(BF16)
    mu_ref[...] = jnp.broadcast_to(mu, mu_ref.shape)
    rstd_ref[...] = jnp.broadcast_to(rstd, rstd_ref.shape)


def _layer_norm(s, g, b, *, tm):
    T, D = s.shape
    blocks = [_nbytes((tm, D), F32), _nbytes((tm, D), BF16), 2 * _nbytes((1, D), F32), 2 * _nbytes((tm, LANE), F32)]
    row = pl.BlockSpec((tm, D), lambda i: (i, 0))
    vec = pl.BlockSpec((1, D), lambda i: (0, 0))
    stat = pl.BlockSpec((tm, LANE), lambda i: (i, 0))
    g2, b2 = g.reshape(1, D), b.reshape(1, D)
    hb, mu, rstd = pl.pallas_call(
        _ln_kernel,
        grid=(T // tm,),
        in_specs=[row, vec, vec],
        out_specs=[row, stat, stat],
        out_shape=[jax.ShapeDtypeStruct((T, D), BF16)] + [jax.ShapeDtypeStruct((T, LANE), F32)] * 2,
        compiler_params=_params(1, blocks, 2 * _nbytes((tm, D), F32)),
        name="layer_norm",
    )(s, g2, b2)
    return hb, (s, mu, rstd, g2, b2)


def _swiglu_kernel(h_ref, wg_ref, wu_ref, o_ref):
    wg = wg_ref[...].astype(BF16)
    wu = wu_ref[...].astype(BF16)
    tm = o_ref.shape[0]
    sub = min(tm, MM_SUB_ROWS)
    for r in range(0, tm, sub):
        h = h_ref[r:r + sub, :]
        gate = _dot(h, wg)
        up = _dot(h, wu)
        o_ref[r:r + sub, :] = (gate * jax.nn.sigmoid(gate) * up).astype(o_ref.dtype)


def _swiglu(h, w_gate_up, d_ff, *, tm, tn):
    T, K = h.shape
    n_j = d_ff // tn
    blocks = [_nbytes((tm, K), BF16), 2 * _nbytes((K, tn), w_gate_up.dtype), _nbytes((tm, tn), BF16)]
    return pl.pallas_call(
        _swiglu_kernel,
        grid=(T // tm, n_j),
        in_specs=[pl.BlockSpec((tm, K), lambda i, j: (i, 0)),
                  pl.BlockSpec((K, tn), lambda i, j: (0, j)),
                  pl.BlockSpec((K, tn), lambda i, j: (0, j + n_j))],
        out_specs=pl.BlockSpec((tm, tn), lambda i, j: (i, j)),
        out_shape=jax.ShapeDtypeStruct((T, d_ff), BF16),
        compiler_params=_params(2, blocks, 3 * _nbytes((tm, tn), F32)),
        name="swiglu",
    )(h, w_gate_up, w_gate_up)


def _ple_kernel(hb_ref, wg_ref, p_ref, wp_ref, s_ref, mu_ref, rstd_ref, g_ref, b_ref, y_ref):
    wg = wg_ref[...].astype(BF16)
    wp = wp_ref[...]
    tm = y_ref.shape[0]
    sub = min(tm, MM_SUB_ROWS)
    for r in range(0, tm, sub):
        rows = slice(r, r + sub)
        h = _ln_apply(s_ref[rows, :], mu_ref[rows, :], rstd_ref[rows, :], g_ref[...], b_ref[...])
        gate = jax.nn.sigmoid(_dot(hb_ref[rows, :], wg))
        y_ref[rows, :] = h + gate * _dot(p_ref[rows, :], wp)


def _ple(hb, h_ln, p, w_gate, w_proj, *, tm, tn):
    T, D = hb.shape
    P = p.shape[1]
    blocks = [_nbytes((tm, D), BF16), _nbytes((D, tn), w_gate.dtype), _nbytes((tm, P), BF16),
              _nbytes((P, tn), BF16), 2 * _nbytes((tm, tn), F32), 2 * _nbytes((tm, LANE), F32)]
    return pl.pallas_call(
        _ple_kernel,
        grid=(T // tm, D // tn),
        in_specs=[pl.BlockSpec((tm, D), lambda i, j: (i, 0)),
                  pl.BlockSpec((D, tn), lambda i, j: (0, j)),
                  pl.BlockSpec((tm, P), lambda i, j: (i, 0)),
                  pl.BlockSpec((P, tn), lambda i, j: (0, j))] + _ln_tile_specs(tm, tn),
        out_specs=pl.BlockSpec((tm, tn), lambda i, j: (i, j)),
        out_shape=jax.ShapeDtypeStruct((T, D), F32),
        compiler_params=_params(2, blocks, 3 * _nbytes((tm, tn), F32)),
        name="ple",
    )(hb, w_gate, p, w_proj, *h_ln)


def _pool_kernel(u_ref, halo_ref, w_ref, scale_ref, o_ref, d_ref, *, bb, L, tl, start):
    P = u_ref.shape[1]
    G = len(POOL_WINDOWS)
    gw = P // G
    n_sub = L // tl
    row = lax.broadcasted_iota(jnp.int32, (tl, 1), 0)

    def sub_tile(bi, t, tail):
        r0 = pl.multiple_of(bi * L + t * tl, SUBLANE)
        x = u_ref[pl.ds(r0, tl), :]
        ext = jnp.concatenate([tail, x], axis=0)
        pos = start + t * tl + row
        for gi, win in enumerate(POOL_WINDOWS):
            cols = slice(gi * gw, (gi + 1) * gw)
            s = ext[:, cols]
            shift = 1
            while shift < win:
                s = s + pltpu.roll(s, shift, axis=0)
                shift *= 2
            cnt = jnp.minimum(pos + 1, win).astype(F32)
            d = s[POOL_HALO:] / cnt - x[:, cols]
            d_ref[pl.ds(r0, tl), cols] = d
        return ext[tl:]

    def seq(bi, carry):
        tail0 = halo_ref[bi]
        if n_sub == 1:
            sub_tile(bi, 0, tail0)
        else:
            lax.fori_loop(0, n_sub, lambda t, tail: sub_tile(bi, t, tail), tail0)
        return carry

    lax.fori_loop(0, bb, seq, 0)
    for gi in range(G):
        cols = slice(gi * gw, (gi + 1) * gw)
        y = _dot(d_ref[:, cols].astype(BF16), w_ref[gi])
        o_ref[:, cols] = (y * scale_ref[:, cols]).astype(o_ref.dtype)


def _pool_mixer(proj, halo, w_pool, pool_scale, *, bb, L, start):
    B, _, P = halo.shape
    G, gw, _ = w_pool.shape
    rows = bb * L
    tl = min(L, 256)
    blocks = [_nbytes((rows, P), F32), _nbytes((bb, POOL_HALO, P), F32), _nbytes(w_pool.shape, BF16),
              _nbytes((1, P), F32), _nbytes((rows, P), BF16)]
    return pl.pallas_call(
        functools.partial(_pool_kernel, bb=bb, L=L, tl=tl, start=start),
        grid=(B // bb,),
        in_specs=[pl.BlockSpec((rows, P), lambda i: (i, 0)),
                  pl.BlockSpec((bb, POOL_HALO, P), lambda i: (i, 0, 0)),
                  pl.BlockSpec((G, gw, gw), lambda i: (0, 0, 0)),
                  pl.BlockSpec((1, P), lambda i: (0, 0))],
        out_specs=pl.BlockSpec((rows, P), lambda i: (i, 0)),
        out_shape=jax.ShapeDtypeStruct((B * L, P), BF16),
        scratch_shapes=[pltpu.VMEM((rows, P), F32)],
        compiler_params=_params(1, blocks, _nbytes((rows, P), F32) + 4 * _nbytes((tl + POOL_HALO, P), F32)),
        name="pool_mixer",
    )(proj, halo, w_pool, pool_scale.reshape(1, P))


def _gates_kernel(x_ref, w_ref, alog_ref, dtb_ref, xb_ref, beta_ref, gc_ref, gl_ref, *, unit):
    xb = x_ref[...].astype(BF16)
    xb_ref[...] = xb
    t = lax.dot_general(xb, w_ref[...], (((1,), (1,)), ((), ())),
                        preferred_element_type=F32)
    beta_ref[...] = jax.nn.sigmoid(t)
    x = t + dtb_ref[...]
    softplus = jnp.maximum(x, 0.0) + jnp.log1p(jnp.exp(-jnp.abs(x)))
    g = -jnp.exp(alog_ref[...]) * softplus
    rows = g.shape[0]
    pos = lax.broadcasted_iota(jnp.int32, g.shape, 0) % unit
    shift = 1
    while shift < unit:
        g = g + jnp.where(pos >= shift, pltpu.roll(g, shift, axis=0), 0.0)
        shift *= 2
    gc_ref[...] = g
    shift = 1
    while shift < unit:
        g = jnp.where((pos & shift) == 0, pltpu.roll(g, rows - shift, axis=0), g)
        shift *= 2
    gl_ref[...] = g


def _gates(x, w_tail, alog_row, dtb_row, *, unit, tm):
    T, D = x.shape
    spec = pl.BlockSpec((tm, LANE), lambda i: (i, 0))
    vec = pl.BlockSpec((1, LANE), lambda i: (0, 0))
    row = pl.BlockSpec((tm, D), lambda i: (i, 0))
    blocks = [_nbytes((tm, D), F32), _nbytes((tm, D), BF16), _nbytes((LANE, D), BF16), 3 * _nbytes((tm, LANE), F32)]
    return pl.pallas_call(
        functools.partial(_gates_kernel, unit=unit),
        grid=(T // tm,),
        in_specs=[row, pl.BlockSpec((LANE, D), lambda i: (0, 0)), vec, vec],
        out_specs=[row, spec, spec, spec],
        out_shape=[jax.ShapeDtypeStruct((T, D), BF16)] + [jax.ShapeDtypeStruct((T, LANE), F32)] * 3,
        compiler_params=_params(1, blocks, 8 * _nbytes((tm, LANE), F32)),
        name="gates",
    )(x, w_tail, alog_row, dtb_row)


def _zmap(fn, *lists):
    return [fn(*args) for args in zip(*lists)]


def _block_terms(qs, ks, vs, cfs, his, grows, brows, masks, n_fac):
    incl, strict, eye = masks
    C, D = qs[0].shape
    bcols = [cf[:, CF_BETA + hi:CF_BETA + hi + 1] for cf, hi in zip(cfs, his)]
    gcols = [cf[:, CF_GC + hi:CF_GC + hi + 1] for cf, hi in zip(cfs, his)]
    glcols = [cf[:, CF_GLAST + hi:CF_GLAST + hi + 1] for cf, hi in zip(cfs, his)]
    decays = _zmap(lambda gc, gr: jnp.where(incl, jnp.exp(jnp.where(incl, gc - gr, 0.0)), 0.0), gcols, grows)
    qk2s = _zmap(lambda q, k: lax.dot_general(jnp.concatenate([q, k], axis=0), k, (((1,), (1,)), ((), ())),
                                              preferred_element_type=F32), qs, ks)
    a_s = _zmap(lambda qk2, dc, bc: jnp.where(strict, qk2[C:] * dc * bc, 0.0), qk2s, decays, bcols)
    ps = [jnp.where(eye, 1.0, 0.0) - a for a in a_s]
    if n_fac > 0:
        sqs = [_dot(a, a) for a in a_s]
        for f in range(n_fac):
            if f < n_fac - 1:
                rs = _zmap(lambda p, sq: _dot(jnp.concatenate([p, sq], axis=0), sq), ps, sqs)
                ps = _zmap(lambda p, r: p + r[:C], ps, rs)
                sqs = [r[C:] for r in rs]
            else:
                ps = _zmap(lambda p, sq: p + _dot(p, sq), ps, sqs)
    tbs = _zmap(lambda p, br: p * br, ps, brows)
    w1s = _zmap(lambda tb, gr, k: _dot(tb * jnp.exp(gr), k), tbs, grows, ks)
    u1s = _zmap(_dot, tbs, vs)
    wus = _zmap(lambda w1, u1: jnp.concatenate([w1, u1], axis=1), w1s, u1s)
    tops = _zmap(lambda qk2, dc, wu: _dot(qk2[:C] * dc, wu), qk2s, decays, wus)
    q_effs = _zmap(lambda q, gc, top: q * jnp.exp(gc) - top[:, :D], qs, gcols, tops)
    k_decs = _zmap(lambda k, gl, gc: k * jnp.exp(gl - gc), ks, glcols, gcols)
    return wus, q_effs, [top[:, D:] for top in tops], k_decs


def _conv_silu(xs, shifteds, w_ref, colss):
    ys = _zmap(lambda x, cols: x * w_ref[3:4, cols], xs, colss)
    for s in (1, 2, 3):
        ys = _zmap(lambda y, sh, cols: y + sh(s) * w_ref[3 - s:4 - s, cols], ys, shifteds, colss)
    return [y * jax.nn.sigmoid(y) for y in ys]


def _l2norm(x, scale=1.0):
    return x * (lax.rsqrt(jnp.sum(x * x, -1, keepdims=True) + L2_EPS) * scale)


def _finish(o, z, ong):
    o = o * lax.rsqrt(jnp.mean(o * o, -1, keepdims=True) + RMS_EPS) * ong
    return o * (z * jax.nn.sigmoid(z))


def _masks(C, unit):
    ri = lax.broadcasted_iota(jnp.int32, (C, C), 0)
    ci = lax.broadcasted_iota(jnp.int32, (C, C), 1)
    same = (ri // unit) == (ci // unit)
    return same & (ri >= ci), same & (ri > ci), ri == ci


def _delta_seq_kernel(q_ref, k_ref, v_ref, z_ref, pq_ref, pk_ref, pv_ref, hq_ref, hk_ref, hv_ref,
                      wq_ref, wk_ref, wv_ref, gch_ref, bth_ref, cf_ref, ong_ref, o_ref, s_ref,
                      lhs_scr, n_scr, o0_scr, *, hb, L, unroll):
    C, D = CHUNK, LANE
    N = L // C
    masks = _masks(C, C)
    n_fac = int(math.log2(C)) - 1
    ong = ong_ref[...]
    first_tile = pl.program_id(2) == 0

    @pl.when(first_tile)
    def _():
        for hi in range(hb):
            s_ref[0, hi] = jnp.zeros((D, D), F32)

    def group(g):
        return [(g * unroll + u, hi) for u in range(unroll) for hi in range(hb)]

    def prep(g):
        items = group(g)
        colss = [slice(hi * D, (hi + 1) * D) for _, hi in items]

        def conv(x_ref, p_ref, h_ref, w_ref):
            xs = [x_ref[n * C:(n + 1) * C, cols] for (n, _), cols in zip(items, colss)]
            prevs = [x_ref[n * C - CONV_HALO:n * C, cols] if n > 0
                     else jnp.where(first_tile, h_ref[0, :, cols], p_ref[:, cols])
                     for (n, _), cols in zip(items, colss)]
            exts = _zmap(lambda prev, x: jnp.concatenate([prev, x], axis=0), prevs, xs)
            shifteds = [lambda s, ext=ext: pltpu.roll(ext, s, axis=0)[CONV_HALO:] for ext in exts]
            return _conv_silu(xs, shifteds, w_ref, colss)

        qs = [_l2norm(y, D ** -0.5) for y in conv(q_ref, pq_ref, hq_ref, wq_ref)]
        ks = [_l2norm(y) for y in conv(k_ref, pk_ref, hk_ref, wk_ref)]
        return qs, ks, conv(v_ref, pv_ref, hv_ref, wv_ref)

    def terms(g, qkv):
        items = group(g)
        his = [hi for _, hi in items]
        cfs = [cf_ref[0, n * C:(n + 1) * C, :] for n, _ in items]
        grows = [gch_ref[0, hi, n:n + 1, :] for n, hi in items]
        brows = [bth_ref[0, hi, n:n + 1, :] for n, hi in items]
        wus, q_effs, o0s, k_decs = _block_terms(*qkv, cfs, his, grows, brows, masks, n_fac)
        mns = _zmap(_dot_t0, k_decs, wus)
        for (n, hi), mn, q_eff, o0 in zip(items, mns, q_effs, o0s):
            lhs_scr[hi, n, :D, :] = mn[:, :D]
            lhs_scr[hi, n, D:, :] = q_eff
            n_scr[hi, n] = mn[:, D:]
            o0_scr[hi, n] = o0

    n_groups = N // unroll
    qkv = prep(0)
    for g in range(n_groups):
        nxt = prep(g + 1) if g + 1 < n_groups else None
        terms(g, qkv)
        qkv = nxt

    heads = list(range(hb))
    states = [s_ref[0, hi] for hi in heads]
    for n in range(N):
        rs = [_dot(lhs_scr[hi, n], s) for hi, s in zip(heads, states)]
        states = [s * jnp.exp(gch_ref[0, hi, n:n + 1, :][:, C - 1:C]) - r[:D] + n_scr[hi, n]
                  for hi, s, r in zip(heads, states, rs)]
        for hi, r in zip(heads, rs):
            cols = slice(hi * D, (hi + 1) * D)
            o = o0_scr[hi, n] + r[D:]
            o_ref[n * C:(n + 1) * C, cols] = _finish(o, z_ref[n * C:(n + 1) * C, cols], ong).astype(o_ref.dtype)
    for hi, s in zip(heads, states):
        s_ref[0, hi] = s


def _delta_units_kernel(q_ref, k_ref, v_ref, z_ref, hq_ref, hk_ref, hv_ref, wq_ref, wk_ref, wv_ref,
                        gch_ref, bth_ref, cf_ref, s0_ref, ong_ref, o_ref, s_ref, *, hb, unit):
    C, D = CHUNK, LANE
    n_units = C // unit
    masks = _masks(C, unit)
    n_fac = max(int(math.log2(unit)) - 1, 0)
    ong = ong_ref[...]
    pos = lax.broadcasted_iota(jnp.int32, (C, 1), 0) % unit
    his = list(range(hb))
    colss = [slice(hi * D, (hi + 1) * D) for hi in his]

    def conv(x_ref, h_ref, w_ref):
        xs = [x_ref[:, cols] for cols in colss]
        halos = [h_ref[:, cols] for cols in colss]
        shifteds = [lambda s, x=x, halo=halo: jnp.where(pos >= s, pltpu.roll(x, s, axis=0),
                                                        pltpu.roll(halo, C + s - CONV_HALO, axis=0))
                    for x, halo in zip(xs, halos)]
        return _conv_silu(xs, shifteds, w_ref, colss)

    qs = [_l2norm(y, D ** -0.5) for y in conv(q_ref, hq_ref, wq_ref)]
    ks = [_l2norm(y) for y in conv(k_ref, hk_ref, wk_ref)]
    vs = conv(v_ref, hv_ref, wv_ref)
    cf = cf_ref[0]
    wus, q_effs, o0s, k_decs = _block_terms(qs, ks, vs, [cf] * hb, his, [gch_ref[0, hi] for hi in his],
                                            [bth_ref[0, hi] for hi in his], masks, n_fac)
    units = [(hi, u) for u in range(n_units) for hi in his]
    rowss = [slice(u * unit, (u + 1) * unit) for _, u in units]
    s_olds = [s0_ref[u, hi] for hi, u in units]
    xs = [_dot(jnp.concatenate([wus[hi][rows, :D], q_effs[hi][rows]], axis=0), s_old)
          for (hi, _), rows, s_old in zip(units, rowss, s_olds)]
    upds = [_dot_t0(k_decs[hi][rows], wus[hi][rows, D:] - x[:unit]) for (hi, _), rows, x in zip(units, rowss, xs)]
    for (hi, u), s_old, upd in zip(units, s_olds, upds):
        glast = cf[u * unit:u * unit + 1, CF_GLAST + hi:CF_GLAST + hi + 1]
        s_ref[u, hi] = s_old * jnp.exp(glast) + upd
    for hi in his:
        o = jnp.concatenate([o0s[hi][rows] + x[unit:] for (h2, _), rows, x in zip(units, rowss, xs) if h2 == hi],
                            axis=0)
        o_ref[:, colss[hi]] = _finish(o, z_ref[:, colss[hi]], ong).astype(o_ref.dtype)


def _delta_seq(proj, offs, conv_halo, w_conv8, gc_hm, beta_hm, cf, o_norm_g, *, hb, lt, unroll):
    T = proj.shape[0]
    B, H, N, _ = gc_hm.shape
    D = LANE
    W = hb * D
    L = T // B
    n_t = L // lt
    nt = lt // CHUNK
    hpt = lt // CONV_HALO

    def col_spec(off):
        return pl.BlockSpec((lt, W), lambda i, j, t, o=off // W: (i * n_t + t, o + j))

    def prev_spec(off):
        return pl.BlockSpec((CONV_HALO, W),
                            lambda i, j, t, o=off // W: (jnp.maximum((i * n_t + t) * hpt - 1, 0), o + j))

    def halo_spec(g):
        return pl.BlockSpec((1, CONV_HALO, W), lambda i, j, t, o=g * (H // hb): (i, 0, o + j))

    def w_spec(g):
        return pl.BlockSpec((CONV_HALO, W), lambda i, j, t, o=g * (H // hb): (0, o + j))

    hm_spec = pl.BlockSpec((1, hb, nt, CHUNK), lambda i, j, t: (i, j, t, 0))
    state_spec = pl.BlockSpec((1, hb, D, D), lambda i, j, t: (i, j, 0, 0))
    in_specs = ([col_spec(o) for o in offs] + [prev_spec(o) for o in offs[:3]] + [halo_spec(g) for g in range(3)]
                + [w_spec(g) for g in range(3)]
                + [hm_spec, hm_spec, pl.BlockSpec((1, lt, LANE), lambda i, j, t: (j, i * n_t + t, 0)),
                   pl.BlockSpec((1, D), lambda i, j, t: (0, 0))])
    args = [proj] * 7 + [conv_halo] * 3 + [w_conv8] * 3 + [gc_hm, beta_hm, cf, o_norm_g.reshape(1, D)]
    blocks = [4 * _nbytes((lt, W), F32), 9 * _nbytes((CONV_HALO, W), F32),
              2 * _nbytes((hb, max(nt, SUBLANE), LANE), F32), _nbytes((lt, LANE), F32),
              _nbytes((hb, D, D), F32), _nbytes((lt, W), BF16)]
    scratch = [pltpu.VMEM((hb, nt, D + CHUNK, D), F32), pltpu.VMEM((hb, nt, D, D), F32),
               pltpu.VMEM((hb, nt, CHUNK, D), F32)]
    return pl.pallas_call(
        functools.partial(_delta_seq_kernel, hb=hb, L=lt, unroll=unroll),
        grid=(B, H // hb, n_t),
        in_specs=in_specs,
        out_specs=[pl.BlockSpec((lt, W), lambda i, j, t: (i * n_t + t, j)), state_spec],
        out_shape=[jax.ShapeDtypeStruct((T, H * D), BF16), jax.ShapeDtypeStruct((B, H, D, D), F32)],
        scratch_shapes=scratch,
        compiler_params=_params(3, blocks, hb * nt * _nbytes((2 * D + 2 * CHUNK, D), F32)),
        name="delta_seq",
    )(*args)


def _delta_units(proj, offs, conv_halo, w_conv8, gc_hm, beta_hm, cf, s0, o_norm_g, *, unit, hb):
    T = proj.shape[0]
    nb, H, _, _ = gc_hm.shape
    D = LANE
    W = hb * D
    n_units = CHUNK // unit

    def col_spec(off):
        return pl.BlockSpec((CHUNK, W), lambda i, j, o=off // W: (i, o + j))

    def halo_spec(g):
        return pl.BlockSpec((CHUNK, W), lambda i, j, o=g * (H // hb): (i, o + j))

    def w_spec(g):
        return pl.BlockSpec((CONV_HALO, W), lambda i, j, o=g * (H // hb): (0, o + j))

    hm_spec = pl.BlockSpec((1, hb, 1, CHUNK), lambda i, j: (i, j, 0, 0))
    state_spec = pl.BlockSpec((n_units, hb, D, D), lambda i, j: (i, j, 0, 0))
    in_specs = ([col_spec(o) for o in offs] + [halo_spec(g) for g in range(3)] + [w_spec(g) for g in range(3)]
                + [hm_spec, hm_spec, pl.BlockSpec((1, CHUNK, LANE), lambda i, j: (j, i, 0)), state_spec,
                   pl.BlockSpec((1, D), lambda i, j: (0, 0))])
    args = [proj] * 4 + [conv_halo] * 3 + [w_conv8] * 3 + [gc_hm, beta_hm, cf, s0, o_norm_g.reshape(1, D)]
    blocks = [7 * _nbytes((CHUNK, W), F32), 3 * _nbytes((CONV_HALO, W), F32),
              2 * _nbytes((hb, SUBLANE, LANE), F32), _nbytes((CHUNK, LANE), F32),
              2 * _nbytes((n_units, hb, D, D), F32), _nbytes((CHUNK, W), BF16)]
    return pl.pallas_call(
        functools.partial(_delta_units_kernel, hb=hb, unit=unit),
        grid=(nb, H // hb),
        in_specs=in_specs,
        out_specs=[pl.BlockSpec((CHUNK, W), lambda i, j: (i, j)), state_spec],
        out_shape=[jax.ShapeDtypeStruct((T, H * D), BF16),
                   jax.ShapeDtypeStruct((nb * n_units, H, D, D), F32)],
        compiler_params=_params(2, blocks),
        name="delta_units",
    )(*args)


def _head_major(tok, nb, H, lane0):
    T = tok.shape[0]
    x = tok[:, lane0:lane0 + H].reshape(nb, T // nb // CHUNK, CHUNK, H)
    return jnp.transpose(x, (0, 3, 1, 2))


def _col_form(beta_tok, gc_tok, gl_tok, H, hb):
    T = beta_tok.shape[0]

    def part(tok, lane0, width):
        x = tok[:, lane0:lane0 + H].reshape(T, H // hb, hb)
        x = jnp.transpose(x, (1, 0, 2))
        return jnp.pad(x, ((0, 0), (0, 0), (0, width - hb)))

    return jnp.concatenate([part(beta_tok, 0, CF_GC - CF_BETA), part(gc_tok, H, CF_GLAST - CF_GC),
                            part(gl_tok, H, LANE - CF_GLAST)], axis=-1)


def _tail_rows(state, new, n):
    L = new.shape[1]
    if L >= n:
        return new[:, L - n:]
    return jnp.concatenate([state[:, state.shape[1] - (n - L):], new], axis=1)


def _layer(x, p, pool_state, conv_state, delta_state, start, lw, cfg):
    (w_in_t, w_tail_b, w_pool_b, pool_scale, w_conv8, alog_row, dtb_row, o_norm_g, w_out, ln1_g, ln1_b,
     w_gate_up, w_down_b, ln2_g, ln2_b, w_ple_gate, w_ple_proj_b, alpha) = lw
    B, L, D = x.shape
    T = B * L
    P, DN, H, d_ff = cfg["P"], cfg["DN"], cfg["H"], cfg["d_ff"]
    n_main = P + 4 * DN
    tm = _pick(T, (1024, 512, 256, 128))
    tm_big = _pick(T, (2048, 1024, 512, 256, 128))
    tm_ln = min(tm, 512)
    tm_down = min(tm, 512)
    tn_ff = _pick(d_ff, (256, 128))
    x2 = x.reshape(T, D)
    unit = math.gcd(L, CHUNK)
    xb, beta_tok, gc_tok, gl_tok = _gates(x2, w_tail_b, alog_row, dtb_row, unit=unit, tm=min(tm, 512))

    proj = _matmul_nt(xb, w_in_t, n_main, tm=tm_big, tn=_pick(n_main, (256, 128)), out_dtype=F32)
    proj3 = proj.reshape(B, L, n_main)
    n_pool, n_conv = pool_state.shape[1], conv_state.shape[1]
    new_pool = _tail_rows(pool_state, proj3[:, :, :P], n_pool)
    new_conv = _tail_rows(conv_state, proj3[:, :, P:P + 3 * DN], n_conv)

    pool_halo = jnp.pad(pool_state, ((0, 0), (POOL_HALO - n_pool, 0), (0, 0)))
    bb_pool = 1 if L >= 256 else _pick(B, (16, 8, 4, 2, 1))
    y_pool = _pool_mixer(proj, pool_halo, w_pool_b, pool_scale, bb=bb_pool, L=L, start=start)

    conv_halo = jnp.pad(conv_state, ((0, 0), (CONV_HALO - n_conv, 0), (0, 0)))
    offs = (P, P + DN, P + 2 * DN, P + 3 * DN)
    if unit == CHUNK:
        assert delta_state is None
        hb = _pick(H, (8, 4, 2, 1))
        o, s_new = _delta_seq(proj, offs, conv_halo, w_conv8, _head_major(gc_tok, B, H, H),
                              _head_major(beta_tok, B, H, 0), _col_form(beta_tok, gc_tok, gl_tok, H, hb),
                              o_norm_g, hb=hb, lt=_pick(L, (512, 256, 128, 64)), unroll=2)
    else:
        assert unit == CONV_HALO and T % CHUNK == 0
        nb, hb = T // CHUNK, _pick(H, (8, 4, 2, 1))
        o, s_new = _delta_units(proj, offs, conv_halo.reshape(B * CONV_HALO, 3 * DN), w_conv8,
                                _head_major(gc_tok, nb, H, H), _head_major(beta_tok, nb, H, 0),
                                _col_form(beta_tok, gc_tok, gl_tok, H, hb), delta_state, o_norm_g,
                                unit=unit, hb=hb)

    s1 = _matmul_resid((y_pool, o), w_out, x2, alpha, tm=tm_big, tn=tn_ff)
    h1b, h1_ln = _layer_norm(s1, ln1_g, ln1_b, tm=tm_ln)
    act = _swiglu(h1b, w_gate_up, d_ff, tm=tm_big, tn=tn_ff)
    s2 = _matmul_resid((act,), w_down_b, h1_ln, alpha, tm=tm_down, tn=_pick(D, (512, 256, 128)))
    h2b, h2_ln = _layer_norm(s2, ln2_g, ln2_b, tm=tm_ln)
    y = _ple(h2b, h2_ln, p.reshape(T, -1).astype(BF16), w_ple_gate, w_ple_proj_b, tm=tm_big, tn=tn_ff)
    return y.reshape(B, L, D), new_pool, new_conv, s_new


def kernel(x_prompt, x_sample, state_pool, state_conv, state_delta, p_prompt, p_sample, w_in, w_pool, pool_scale, w_conv, a_log, dt_bias, o_norm_g, w_out, ln1_g, ln1_b, w_gate_up, w_down, ln2_g, ln2_b, w_ple_gate, w_ple_proj):
    depth = w_in.shape[0]
    B = x_prompt.shape[0]
    D = x_prompt.shape[2]
    P = w_pool.shape[1] * w_pool.shape[2]
    H = a_log.shape[1]
    DN = H * LANE
    n_main = P + 4 * DN
    cfg = dict(P=P, DN=DN, H=H, d_ff=w_down.shape[1])
    alpha = (2.0 * depth) ** 0.25
    assert w_in.shape[2] == n_main + 2 * H and H <= CF_GC - CF_BETA and n_main % LANE == 0
    assert w_conv.shape[1] == 4 and D == P + DN and max(POOL_WINDOWS) - 1 == state_pool.shape[2]
    assert o_norm_g.shape[1] == LANE

    yp, ys = x_prompt, x_sample
    outs = [[] for _ in range(6)]
    for li in range(depth):
        w_in_t = jnp.swapaxes(w_in[li], 0, 1)
        lw = (w_in_t,
              jnp.pad(w_in_t[n_main:], ((0, LANE - 2 * H), (0, 0))).astype(BF16),
              w_pool[li].astype(BF16), pool_scale[li],
              jnp.pad(w_conv[li], ((0, CONV_HALO - w_conv.shape[1]), (0, 0))),
              jnp.pad(a_log[li], (H, LANE - 2 * H)).reshape(1, LANE),
              jnp.pad(dt_bias[li], (H, LANE - 2 * H)).reshape(1, LANE),
              o_norm_g[li], w_out[li], ln1_g[li], ln1_b[li], w_gate_up[li],
              w_down[li].astype(BF16), ln2_g[li], ln2_b[li], w_ple_gate[li],
              w_ple_proj[li].astype(BF16), alpha)
        zeros_pool = jnp.zeros((B,) + state_pool.shape[2:], F32)
        zeros_conv = jnp.zeros((B,) + state_conv.shape[2:], F32)
        yp, npl, ncv, nst = _layer(yp, p_prompt[li], zeros_pool, zeros_conv, None, 0, lw, cfg)
        ys, spl, scv, sst = _layer(ys, p_sample[li], state_pool[li], state_conv[li], state_delta[li], PAST_LEN,
                                   lw, cfg)
        for acc, val in zip(outs, (npl, ncv, nst, spl, scv, sst)):
            acc.append(val)
    return (yp, ys) + tuple(jnp.stack(o) for o in outs)
```

```python
import functools
import math

import jax
import jax.numpy as jnp
from jax import lax
from jax.experimental import pallas as pl
from jax.experimental.pallas import tpu as pltpu

F32 = jnp.float32
BF16 = jnp.bfloat16

PAST_LEN = 16384
POOL_WINDOWS = (2, 4, 8, 16)
POOL_HALO = 16
CONV_HALO = 8
CHUNK = 64
LN_EPS = 1e-5
RMS_EPS = 1e-6
L2_EPS = 1e-6

LANE = 128
SUBLANE = 8
VMEM_BUDGET = 56 * 1024 * 1024

CF_BETA, CF_GC, CF_GLAST = 0, 32, 64


def _vmem_limit(block_bytes, scratch_bytes=0):
    est = 2 * sum(block_bytes) + scratch_bytes + (12 << 20)
    return int(min(est, VMEM_BUDGET))


def _nbytes(shape, dtype):
    return math.prod(shape) * jnp.dtype(dtype).itemsize


def _pick(n, candidates):
    for c in candidates:
        if n % c == 0:
            return c
    raise ValueError(f"no tile for {n} among {candidates}")


def _params(n_axes, block_bytes, scratch_bytes=0):
    return pltpu.CompilerParams(dimension_semantics=("arbitrary",) * n_axes,
                                vmem_limit_bytes=_vmem_limit(block_bytes, scratch_bytes))


def _dot(a, b):
    return jnp.dot(a, b, preferred_element_type=F32)


def _dot_t0(a, b):
    return lax.dot_general(a, b, (((0,), (0,)), ((), ())), preferred_element_type=F32)


MM_SUB_ROWS = 1024


def _mm_nt_kernel(x_ref, wt_ref, o_ref):
    wt = wt_ref[...].astype(BF16)
    tm = x_ref.shape[0]
    sub = min(tm, MM_SUB_ROWS)
    for r in range(0, tm, sub):
        o_ref[r:r + sub, :] = lax.dot_general(x_ref[r:r + sub, :], wt, (((1,), (1,)), ((), ())),
                                              preferred_element_type=F32).astype(o_ref.dtype)


def _matmul_nt(x, wt, n_cols, *, tm, tn, out_dtype):
    T, K = x.shape
    blocks = [_nbytes((tm, K), x.dtype), _nbytes((tn, K), wt.dtype), _nbytes((tm, tn), out_dtype)]
    return pl.pallas_call(
        _mm_nt_kernel,
        grid=(T // tm, n_cols // tn),
        in_specs=[pl.BlockSpec((tm, K), lambda i, j: (i, 0)),
                  pl.BlockSpec((tn, K), lambda i, j: (j, 0))],
        out_specs=pl.BlockSpec((tm, tn), lambda i, j: (i, j)),
        out_shape=jax.ShapeDtypeStruct((T, n_cols), out_dtype),
        compiler_params=_params(2, blocks, _nbytes((tm, tn), F32) + _nbytes((tn, K), BF16)),
        name="matmul_nt",
    )(x, wt)


def _ln_apply(s, mu, rstd, g, b):
    reps = s.shape[1] // LANE
    mu = jnp.concatenate([mu] * reps, axis=1)
    rstd = jnp.concatenate([rstd] * reps, axis=1)
    return (s - mu) * rstd * g + b


def _ln_tile_specs(tm, tn):
    stat = pl.BlockSpec((tm, LANE), lambda i, j: (i, 0))
    vec = pl.BlockSpec((1, tn), lambda i, j: (0, j))
    return [pl.BlockSpec((tm, tn), lambda i, j: (i, j)), stat, stat, vec, vec]


def _mm_resid_kernel(*refs, alpha, splits, normed):
    x_refs, (w_ref, *r_refs, o_ref) = refs[:len(splits)], refs[len(splits):]
    w = w_ref[...].astype(BF16)
    tm = o_ref.shape[0]
    sub = min(tm, MM_SUB_ROWS)
    for r in range(0, tm, sub):
        rows = slice(r, r + sub)
        if normed:
            s_ref, mu_ref, rstd_ref, g_ref, b_ref = r_refs
            res = _ln_apply(s_ref[rows, :], mu_ref[rows, :], rstd_ref[rows, :], g_ref[...], b_ref[...])
        else:
            res = r_refs[0][rows, :]
        acc = alpha * res
        k0 = 0
        for x_ref, kw in zip(x_refs, splits):
            acc = acc + _dot(x_ref[rows, :], w[k0:k0 + kw, :])
            k0 += kw
        o_ref[rows, :] = acc


def _matmul_resid(xs, w, resid, alpha, *, tm, tn):
    T = xs[0].shape[0]
    splits = tuple(x.shape[1] for x in xs)
    K, N = w.shape
    normed = isinstance(resid, tuple)
    if normed:
        r_args, r_specs = list(resid), _ln_tile_specs(tm, tn)
    else:
        r_args, r_specs = [resid], [pl.BlockSpec((tm, tn), lambda i, j: (i, j))]
    blocks = [_nbytes((tm, K), BF16), _nbytes((K, tn), w.dtype), 2 * _nbytes((tm, tn), F32),
              2 * _nbytes((tm, LANE), F32)]
    return pl.pallas_call(
        functools.partial(_mm_resid_kernel, alpha=alpha, splits=splits, normed=normed),
        grid=(T // tm, N // tn),
        in_specs=[pl.BlockSpec((tm, kw), lambda i, j: (i, 0)) for kw in splits]
        + [pl.BlockSpec((K, tn), lambda i, j: (0, j))] + r_specs,
        out_specs=pl.BlockSpec((tm, tn), lambda i, j: (i, j)),
        out_shape=jax.ShapeDtypeStruct((T, N), F32),
        compiler_params=_params(2, blocks, _nbytes((tm, tn), F32)),
        name="matmul_resid",
    )(*xs, w, *r_args)


def _ln_kernel(s_ref, g_ref, b_ref, hb_ref, mu_ref, rstd_ref):
    s = s_ref[...]
    mu = jnp.mean(s, -1, keepdims=True)
    c = s - mu
    var = jnp.mean(c * c, -1, keepdims=True)
    rstd = lax.rsqrt(var + LN_EPS)
    hb_ref[...] = (c * rstd * g_ref[...] + b_ref[...]).astype(BF16)
    mu_ref[...] = jnp.broadcast_to(mu, mu_ref.shape)
    rstd_ref[...] = jnp.broadcast_to(rstd, rstd_ref.shape)


def _layer_norm(s, g, b, *, tm):
    T, D = s.shape
    blocks = [_nbytes((tm, D), F32), _nbytes((tm, D), BF16), 2 * _nbytes((1, D), F32), 2 * _nbytes((tm, LANE), F32)]
    row = pl.BlockSpec((tm, D), lambda i: (i, 0))
    vec = pl.BlockSpec((1, D), lambda i: (0, 0))
    stat = pl.BlockSpec((tm, LANE), lambda i: (i, 0))
    g2, b2 = g.reshape(1, D), b.reshape(1, D)
    hb, mu, rstd = pl.pallas_call(
        _ln_kernel,
        grid=(T // tm,),
        in_specs=[row, vec, vec],
        out_specs=[row, stat, stat],
        out_shape=[jax.ShapeDtypeStruct((T, D), BF16)] + [jax.ShapeDtypeStruct((T, LANE), F32)] * 2,
        compiler_params=_params(1, blocks, 2 * _nbytes((tm, D), F32)),
        name="layer_norm",
    )(s, g2, b2)
    return hb, (s, mu, rstd, g2, b2)


def _swiglu_kernel(h_ref, wg_ref, wu_ref, o_ref):
    wg = wg_ref[...].astype(BF16)
    wu = wu_ref[...].astype(BF16)
    tm = o_ref.shape[0]
    sub = min(tm, MM_SUB_ROWS)
    for r in range(0, tm, sub):
        h = h_ref[r:r + sub, :]
        gate = _dot(h, wg)
        up = _dot(h, wu)
        o_ref[r:r + sub, :] = (gate * jax.nn.sigmoid(gate) * up).astype(o_ref.dtype)


def _swiglu(h, w_gate_up, d_ff, *, tm, tn):
    T, K = h.shape
    n_j = d_ff // tn
    blocks = [_nbytes((tm, K), BF16), 2 * _nbytes((K, tn), w_gate_up.dtype), _nbytes((tm, tn), BF16)]
    return pl.pallas_call(
        _swiglu_kernel,
        grid=(T // tm, n_j),
        in_specs=[pl.BlockSpec((tm, K), lambda i, j: (i, 0)),
                  pl.BlockSpec((K, tn), lambda i, j: (0, j)),
                  pl.BlockSpec((K, tn), lambda i, j: (0, j + n_j))],
        out_specs=pl.BlockSpec((tm, tn), lambda i, j: (i, j)),
        out_shape=jax.ShapeDtypeStruct((T, d_ff), BF16),
        compiler_params=_params(2, blocks, 3 * _nbytes((tm, tn), F32)),
        name="swiglu",
    )(h, w_gate_up, w_gate_up)


def _ple_kernel(hb_ref, wg_ref, p_ref, wp_ref, s_ref, mu_ref, rstd_ref, g_ref, b_ref, y_ref):
    wg = wg_ref[...].astype(BF16)
    wp = wp_ref[...]
    tm = y_ref.shape[0]
    sub = min(tm, MM_SUB_ROWS)
    for r in range(0, tm, sub):
        rows = slice(r, r + sub)
        h = _ln_apply(s_ref[rows, :], mu_ref[rows, :], rstd_ref[rows, :], g_ref[...], b_ref[...])
        gate = jax.nn.sigmoid(_dot(hb_ref[rows, :], wg))
        y_ref[rows, :] = h + gate * _dot(p_ref[rows, :], wp)


def _ple(hb, h_ln, p, w_gate, w_proj, *, tm, tn):
    T, D = hb.shape
    P = p.shape[1]
    blocks = [_nbytes((tm, D), BF16), _nbytes((D, tn), w_gate.dtype), _nbytes((tm, P), BF16),
              _nbytes((P, tn), BF16), 2 * _nbytes((tm, tn), F32), 2 * _nbytes((tm, LANE), F32)]
    return pl.pallas_call(
        _ple_kernel,
        grid=(T // tm, D // tn),
        in_specs=[pl.BlockSpec((tm, D), lambda i, j: (i, 0)),
                  pl.BlockSpec((D, tn), lambda i, j: (0, j)),
                  pl.BlockSpec((tm, P), lambda i, j: (i, 0)),
                  pl.BlockSpec((P, tn), lambda i, j: (0, j))] + _ln_tile_specs(tm, tn),
        out_specs=pl.BlockSpec((tm, tn), lambda i, j: (i, j)),
        out_shape=jax.ShapeDtypeStruct((T, D), F32),
        compiler_params=_params(2, blocks, 3 * _nbytes((tm, tn), F32)),
        name="ple",
    )(hb, w_gate, p, w_proj, *h_ln)


def _pool_kernel(u_ref, halo_ref, w_ref, scale_ref, o_ref, d_ref, *, bb, L, tl, start):
    P = u_ref.shape[1]
    G = len(POOL_WINDOWS)
    gw = P // G
    n_sub = L // tl
    row = lax.broadcasted_iota(jnp.int32, (tl, 1), 0)

    def sub_tile(bi, t, tail):
        r0 = pl.multiple_of(bi * L + t * tl, SUBLANE)
        x = u_ref[pl.ds(r0, tl), :]
        ext = jnp.concatenate([tail, x], axis=0)
        pos = start + t * tl + row
        for gi, win in enumerate(POOL_WINDOWS):
            cols = slice(gi * gw, (gi + 1) * gw)
            s = ext[:, cols]
            shift = 1
            while shift < win:
                s = s + pltpu.roll(s, shift, axis=0)
                shift *= 2
            cnt = jnp.minimum(pos + 1, win).astype(F32)
            d = s[POOL_HALO:] / cnt - x[:, cols]
            d_ref[pl.ds(r0, tl), cols] = d
        return ext[tl:]

    def seq(bi, carry):
        tail0 = halo_ref[bi]
        if n_sub == 1:
            sub_tile(bi, 0, tail0)
        else:
            lax.fori_loop(0, n_sub, lambda t, tail: sub_tile(bi, t, tail), tail0)
        return carry

    lax.fori_loop(0, bb, seq, 0)
    for gi in range(G):
        cols = slice(gi * gw, (gi + 1) * gw)
        y = _dot(d_ref[:, cols].astype(BF16), w_ref[gi])
        o_ref[:, cols] = (y * scale_ref[:, cols]).astype(o_ref.dtype)


def _pool_mixer(proj, halo, w_pool, pool_scale, *, bb, L, start):
    B, _, P = halo.shape
    G, gw, _ = w_pool.shape
    rows = bb * L
    tl = min(L, 256)
    blocks = [_nbytes((rows, P), F32), _nbytes((bb, POOL_HALO, P), F32), _nbytes(w_pool.shape, BF16),
              _nbytes((1, P), F32), _nbytes((rows, P), BF16)]
    return pl.pallas_call(
        functools.partial(_pool_kernel, bb=bb, L=L, tl=tl, start=start),
        grid=(B // bb,),
        in_specs=[pl.BlockSpec((rows, P), lambda i: (i, 0)),
                  pl.BlockSpec((bb, POOL_HALO, P), lambda i: (i, 0, 0)),
                  pl.BlockSpec((G, gw, gw), lambda i: (0, 0, 0)),
                  pl.BlockSpec((1, P), lambda i: (0, 0))],
        out_specs=pl.BlockSpec((rows, P), lambda i: (i, 0)),
        out_shape=jax.ShapeDtypeStruct((B * L, P), BF16),
        scratch_shapes=[pltpu.VMEM((rows, P), F32)],
        compiler_params=_params(1, blocks, _nbytes((rows, P), F32) + 4 * _nbytes((tl + POOL_HALO, P), F32)),
        name="pool_mixer",
    )(proj, halo, w_pool, pool_scale.reshape(1, P))


def _gates_kernel(x_ref, w_ref, alog_ref, dtb_ref, xb_ref, beta_ref, gc_ref, cf_ref, *, unit, H, hb):
    xb = x_ref[...].astype(BF16)
    xb_ref[...] = xb
    t = lax.dot_general(xb, w_ref[...], (((1,), (1,)), ((), ())),
                        preferred_element_type=F32)
    beta = jax.nn.sigmoid(t)
    beta_ref[...] = beta
    x = t + dtb_ref[...]
    softplus = jnp.maximum(x, 0.0) + jnp.log1p(jnp.exp(-jnp.abs(x)))
    g = -jnp.exp(alog_ref[...]) * softplus
    rows = g.shape[0]
    pos = lax.broadcasted_iota(jnp.int32, g.shape, 0) % unit
    shift = 1
    while shift < unit:
        g = g + jnp.where(pos >= shift, pltpu.roll(g, shift, axis=0), 0.0)
        shift *= 2
    gc = g
    gc_ref[...] = gc
    shift = 1
    while shift < unit:
        g = jnp.where((pos & shift) == 0, pltpu.roll(g, rows - shift, axis=0), g)
        shift *= 2
    lane = lax.broadcasted_iota(jnp.int32, g.shape, 1)
    for j in range(H // hb):
        parts = ((CF_BETA, beta, 0), (CF_GC, gc, H), (CF_GLAST, g, H))
        cf = jnp.zeros_like(g)
        for lane0, val, src0 in parts:
            moved = pltpu.roll(val, (lane0 - src0 - j * hb) % LANE, axis=1)
            cf = jnp.where((lane >= lane0) & (lane < lane0 + hb), moved, cf)
        cf_ref[j] = cf


def _gates(x, w_tail, alog_row, dtb_row, *, unit, tm, H, hb):
    T, D = x.shape
    nj = H // hb
    spec = pl.BlockSpec((tm, LANE), lambda i: (i, 0))
    vec = pl.BlockSpec((1, LANE), lambda i: (0, 0))
    row = pl.BlockSpec((tm, D), lambda i: (i, 0))
    blocks = [_nbytes((tm, D), F32), _nbytes((tm, D), BF16), _nbytes((LANE, D), BF16),
              (2 + nj) * _nbytes((tm, LANE), F32)]
    return pl.pallas_call(
        functools.partial(_gates_kernel, unit=unit, H=H, hb=hb),
        grid=(T // tm,),
        in_specs=[row, pl.BlockSpec((LANE, D), lambda i: (0, 0)), vec, vec],
        out_specs=[row, spec, spec, pl.BlockSpec((nj, tm, LANE), lambda i: (0, i, 0))],
        out_shape=[jax.ShapeDtypeStruct((T, D), BF16)] + [jax.ShapeDtypeStruct((T, LANE), F32)] * 2
        + [jax.ShapeDtypeStruct((nj, T, LANE), F32)],
        compiler_params=_params(1, blocks, 8 * _nbytes((tm, LANE), F32)),
        name="gates",
    )(x, w_tail, alog_row, dtb_row)


def _zmap(fn, *lists):
    return [fn(*args) for args in zip(*lists)]


def _block_terms(qs, ks, vs, cfs, his, grows, brows, masks, n_fac):
    incl, strict, eye = masks
    C, D = qs[0].shape
    bcols = [cf[:, CF_BETA + hi:CF_BETA + hi + 1] for cf, hi in zip(cfs, his)]
    gcols = [cf[:, CF_GC + hi:CF_GC + hi + 1] for cf, hi in zip(cfs, his)]
    glcols = [cf[:, CF_GLAST + hi:CF_GLAST + hi + 1] for cf, hi in zip(cfs, his)]
    decays = _zmap(lambda gc, gr: jnp.where(incl, jnp.exp(jnp.where(incl, gc - gr, 0.0)), 0.0), gcols, grows)
    qk2s = _zmap(lambda q, k: lax.dot_general(jnp.concatenate([q, k], axis=0), k, (((1,), (1,)), ((), ())),
                                              preferred_element_type=F32), qs, ks)
    a_s = _zmap(lambda qk2, dc, bc: jnp.where(strict, qk2[C:] * dc * bc, 0.0), qk2s, decays, bcols)
    ps = [jnp.where(eye, 1.0, 0.0) - a for a in a_s]
    if n_fac > 0:
        sqs = [_dot(a, a) for a in a_s]
        for f in range(n_fac):
            if f < n_fac - 1:
                rs = _zmap(lambda p, sq: _dot(jnp.concatenate([p, sq], axis=0), sq), ps, sqs)
                ps = _zmap(lambda p, r: p + r[:C], ps, rs)
                sqs = [r[C:] for r in rs]
            else:
                ps = _zmap(lambda p, sq: p + _dot(p, sq), ps, sqs)
    tbs = _zmap(lambda p, br: p * br, ps, brows)
    w1s = _zmap(lambda tb, gr, k: _dot(tb * jnp.exp(gr), k), tbs, grows, ks)
    u1s = _zmap(_dot, tbs, vs)
    wus = _zmap(lambda w1, u1: jnp.concatenate([w1, u1], axis=1), w1s, u1s)
    tops = _zmap(lambda qk2, dc, wu: _dot(qk2[:C] * dc, wu), qk2s, decays, wus)
    q_effs = _zmap(lambda q, gc, top: q * jnp.exp(gc) - top[:, :D], qs, gcols, tops)
    k_decs = _zmap(lambda k, gl, gc: k * jnp.exp(gl - gc), ks, glcols, gcols)
    return wus, q_effs, [top[:, D:] for top in tops], k_decs


def _conv_silu(xs, shifteds, w_ref, colss):
    ys = _zmap(lambda x, cols: x * w_ref[3:4, cols], xs, colss)
    for s in (1, 2, 3):
        ys = _zmap(lambda y, sh, cols: y + sh(s) * w_ref[3 - s:4 - s, cols], ys, shifteds, colss)
    return [y * jax.nn.sigmoid(y) for y in ys]


def _l2norm(x, scale=1.0):
    return x * (lax.rsqrt(jnp.sum(x * x, -1, keepdims=True) + L2_EPS) * scale)


def _finish(o, z, ong):
    o = o * lax.rsqrt(jnp.mean(o * o, -1, keepdims=True) + RMS_EPS) * ong
    return o * (z * jax.nn.sigmoid(z))


def _masks(C, unit):
    ri = lax.broadcasted_iota(jnp.int32, (C, C), 0)
    ci = lax.broadcasted_iota(jnp.int32, (C, C), 1)
    same = (ri // unit) == (ci // unit)
    return same & (ri >= ci), same & (ri > ci), ri == ci


def _delta_seq_kernel(q_ref, k_ref, v_ref, z_ref, pq_ref, pk_ref, pv_ref, hq_ref, hk_ref, hv_ref,
                      wq_ref, wk_ref, wv_ref, gch_ref, bth_ref, cf_ref, ong_ref, o_ref, s_ref,
                      lhs_scr, n_scr, o0_scr, *, hb, L, unroll):
    C, D = CHUNK, LANE
    N = L // C
    masks = _masks(C, C)
    n_fac = int(math.log2(C)) - 1
    ong = ong_ref[...]
    first_tile = pl.program_id(2) == 0

    @pl.when(first_tile)
    def _():
        for hi in range(hb):
            s_ref[0, hi] = jnp.zeros((D, D), F32)

    def group(g):
        return [(g * unroll + u, hi) for u in range(unroll) for hi in range(hb)]

    def prep(g):
        items = group(g)
        colss = [slice(hi * D, (hi + 1) * D) for _, hi in items]

        def conv(x_ref, p_ref, h_ref, w_ref):
            xs = [x_ref[n * C:(n + 1) * C, cols] for (n, _), cols in zip(items, colss)]
            prevs = [x_ref[n * C - CONV_HALO:n * C, cols] if n > 0
                     else jnp.where(first_tile, h_ref[0, :, cols], p_ref[:, cols])
                     for (n, _), cols in zip(items, colss)]
            exts = _zmap(lambda prev, x: jnp.concatenate([prev, x], axis=0), prevs, xs)
            shifteds = [lambda s, ext=ext: pltpu.roll(ext, s, axis=0)[CONV_HALO:] for ext in exts]
            return _conv_silu(xs, shifteds, w_ref, colss)

        qs = [_l2norm(y, D ** -0.5) for y in conv(q_ref, pq_ref, hq_ref, wq_ref)]
        ks = [_l2norm(y) for y in conv(k_ref, pk_ref, hk_ref, wk_ref)]
        return qs, ks, conv(v_ref, pv_ref, hv_ref, wv_ref)

    def terms(g, qkv):
        items = group(g)
        his = [hi for _, hi in items]
        cfs = [cf_ref[0, n * C:(n + 1) * C, :] for n, _ in items]
        grows = [gch_ref[0, hi, n:n + 1, :] for n, hi in items]
        brows = [bth_ref[0, hi, n:n + 1, :] for n, hi in items]
        wus, q_effs, o0s, k_decs = _block_terms(*qkv, cfs, his, grows, brows, masks, n_fac)
        mns = _zmap(_dot_t0, k_decs, wus)
        for (n, hi), mn, q_eff, o0 in zip(items, mns, q_effs, o0s):
            lhs_scr[hi, n, :D, :] = mn[:, :D]
            lhs_scr[hi, n, D:, :] = q_eff
            n_scr[hi, n] = mn[:, D:]
            o0_scr[hi, n] = o0

    n_groups = N // unroll
    qkv = prep(0)
    for g in range(n_groups):
        nxt = prep(g + 1) if g + 1 < n_groups else None
        terms(g, qkv)
        qkv = nxt

    heads = list(range(hb))
    states = [s_ref[0, hi] for hi in heads]
    for n in range(N):
        rs = [_dot(lhs_scr[hi, n], s) for hi, s in zip(heads, states)]
        states = [s * jnp.exp(gch_ref[0, hi, n:n + 1, :][:, C - 1:C]) - r[:D] + n_scr[hi, n]
                  for hi, s, r in zip(heads, states, rs)]
        for hi, r in zip(heads, rs):
            cols = slice(hi * D, (hi + 1) * D)
            o = o0_scr[hi, n] + r[D:]
            o_ref[n * C:(n + 1) * C, cols] = _finish(o, z_ref[n * C:(n + 1) * C, cols], ong).astype(o_ref.dtype)
    for hi, s in zip(heads, states):
        s_ref[0, hi] = s


def _delta_units_kernel(q_ref, k_ref, v_ref, z_ref, hq_ref, hk_ref, hv_ref, wq_ref, wk_ref, wv_ref,
                        gch_ref, bth_ref, cf_ref, s0_ref, ong_ref, o_ref, s_ref, *, hb, unit):
    C, D = CHUNK, LANE
    n_units = C // unit
    masks = _masks(C, unit)
    n_fac = max(int(math.log2(unit)) - 1, 0)
    ong = ong_ref[...]
    pos = lax.broadcasted_iota(jnp.int32, (C, 1), 0) % unit
    his = list(range(hb))
    colss = [slice(hi * D, (hi + 1) * D) for hi in his]

    def conv(x_ref, h_ref, w_ref):
        xs = [x_ref[:, cols] for cols in colss]
        halos = [h_ref[:, cols] for cols in colss]
        shifteds = [lambda s, x=x, halo=halo: jnp.where(pos >= s, pltpu.roll(x, s, axis=0),
                                                        pltpu.roll(halo, C + s - CONV_HALO, axis=0))
                    for x, halo in zip(xs, halos)]
        return _conv_silu(xs, shifteds, w_ref, colss)

    qs = [_l2norm(y, D ** -0.5) for y in conv(q_ref, hq_ref, wq_ref)]
    ks = [_l2norm(y) for y in conv(k_ref, hk_ref, wk_ref)]
    vs = conv(v_ref, hv_ref, wv_ref)
    cf = cf_ref[0]
    wus, q_effs, o0s, k_decs = _block_terms(qs, ks, vs, [cf] * hb, his, [gch_ref[0, hi] for hi in his],
                                            [bth_ref[0, hi] for hi in his], masks, n_fac)
    units = [(hi, u) for u in range(n_units) for hi in his]
    rowss = [slice(u * unit, (u + 1) * unit) for _, u in units]
    s_olds = [s0_ref[u, hi] for hi, u in units]
    xs = [_dot(jnp.concatenate([wus[hi][rows, :D], q_effs[hi][rows]], axis=0), s_old)
          for (hi, _), rows, s_old in zip(units, rowss, s_olds)]
    upds = [_dot_t0(k_decs[hi][rows], wus[hi][rows, D:] - x[:unit]) for (hi, _), rows, x in zip(units, rowss, xs)]
    for (hi, u), s_old, upd in zip(units, s_olds, upds):
        glast = cf[u * unit:u * unit + 1, CF_GLAST + hi:CF_GLAST + hi + 1]
        s_ref[u, hi] = s_old * jnp.exp(glast) + upd
    for hi in his:
        o = jnp.concatenate([o0s[hi][rows] + x[unit:] for (h2, _), rows, x in zip(units, rowss, xs) if h2 == hi],
                            axis=0)
        o_ref[:, colss[hi]] = _finish(o, z_ref[:, colss[hi]], ong).astype(o_ref.dtype)


def _delta_seq(proj, offs, conv_halo, w_conv8, gc_hm, beta_hm, cf, o_norm_g, *, hb, lt, unroll):
    T = proj.shape[0]
    B, H, N, _ = gc_hm.shape
    D = LANE
    W = hb * D
    L = T // B
    n_t = L // lt
    nt = lt // CHUNK
    hpt = lt // CONV_HALO

    def col_spec(off):
        return pl.BlockSpec((lt, W), lambda i, j, t, o=off // W: (i * n_t + t, o + j))

    def prev_spec(off):
        return pl.BlockSpec((CONV_HALO, W),
                            lambda i, j, t, o=off // W: (jnp.maximum((i * n_t + t) * hpt - 1, 0), o + j))

    def halo_spec(g):
        return pl.BlockSpec((1, CONV_HALO, W), lambda i, j, t, o=g * (H // hb): (i, 0, o + j))

    def w_spec(g):
        return pl.BlockSpec((CONV_HALO, W), lambda i, j, t, o=g * (H // hb): (0, o + j))

    hm_spec = pl.BlockSpec((1, hb, nt, CHUNK), lambda i, j, t: (i, j, t, 0))
    state_spec = pl.BlockSpec((1, hb, D, D), lambda i, j, t: (i, j, 0, 0))
    in_specs = ([col_spec(o) for o in offs] + [prev_spec(o) for o in offs[:3]] + [halo_spec(g) for g in range(3)]
                + [w_spec(g) for g in range(3)]
                + [hm_spec, hm_spec, pl.BlockSpec((1, lt, LANE), lambda i, j, t: (j, i * n_t + t, 0)),
                   pl.BlockSpec((1, D), lambda i, j, t: (0, 0))])
    args = [proj] * 7 + [conv_halo] * 3 + [w_conv8] * 3 + [gc_hm, beta_hm, cf, o_norm_g.reshape(1, D)]
    blocks = [4 * _nbytes((lt, W), F32), 9 * _nbytes((CONV_HALO, W), F32),
              2 * _nbytes((hb, max(nt, SUBLANE), LANE), F32), _nbytes((lt, LANE), F32),
              _nbytes((hb, D, D), F32), _nbytes((lt, W), BF16)]
    scratch = [pltpu.VMEM((hb, nt, D + CHUNK, D), F32), pltpu.VMEM((hb, nt, D, D), F32),
               pltpu.VMEM((hb, nt, CHUNK, D), F32)]
    return pl.pallas_call(
        functools.partial(_delta_seq_kernel, hb=hb, L=lt, unroll=unroll),
        grid=(B, H // hb, n_t),
        in_specs=in_specs,
        out_specs=[pl.BlockSpec((lt, W), lambda i, j, t: (i * n_t + t, j)), state_spec],
        out_shape=[jax.ShapeDtypeStruct((T, H * D), BF16), jax.ShapeDtypeStruct((B, H, D, D), F32)],
        scratch_shapes=scratch,
        compiler_params=_params(3, blocks, hb * nt * _nbytes((2 * D + 2 * CHUNK, D), F32)),
        name="delta_seq",
    )(*args)


def _delta_units(proj, offs, conv_halo, w_conv8, gc_hm, beta_hm, cf, s0, o_norm_g, *, unit, hb):
    T = proj.shape[0]
    nb, H, _, _ = gc_hm.shape
    D = LANE
    W = hb * D
    n_units = CHUNK // unit

    def col_spec(off):
        return pl.BlockSpec((CHUNK, W), lambda i, j, o=off // W: (i, o + j))

    def halo_spec(g):
        return pl.BlockSpec((CHUNK, W), lambda i, j, o=g * (H // hb): (i, o + j))

    def w_spec(g):
        return pl.BlockSpec((CONV_HALO, W), lambda i, j, o=g * (H // hb): (0, o + j))

    hm_spec = pl.BlockSpec((1, hb, 1, CHUNK), lambda i, j: (i, j, 0, 0))
    state_spec = pl.BlockSpec((n_units, hb, D, D), lambda i, j: (i, j, 0, 0))
    in_specs = ([col_spec(o) for o in offs] + [halo_spec(g) for g in range(3)] + [w_spec(g) for g in range(3)]
                + [hm_spec, hm_spec, pl.BlockSpec((1, CHUNK, LANE), lambda i, j: (j, i, 0)), state_spec,
                   pl.BlockSpec((1, D), lambda i, j: (0, 0))])
    args = [proj] * 4 + [conv_halo] * 3 + [w_conv8] * 3 + [gc_hm, beta_hm, cf, s0, o_norm_g.reshape(1, D)]
    blocks = [7 * _nbytes((CHUNK, W), F32), 3 * _nbytes((CONV_HALO, W), F32),
              2 * _nbytes((hb, SUBLANE, LANE), F32), _nbytes((CHUNK, LANE), F32),
              2 * _nbytes((n_units, hb, D, D), F32), _nbytes((CHUNK, W), BF16)]
    return pl.pallas_call(
        functools.partial(_delta_units_kernel, hb=hb, unit=unit),
        grid=(nb, H // hb),
        in_specs=in_specs,
        out_specs=[pl.BlockSpec((CHUNK, W), lambda i, j: (i, j)), state_spec],
        out_shape=[jax.ShapeDtypeStruct((T, H * D), BF16),
                   jax.ShapeDtypeStruct((nb * n_units, H, D, D), F32)],
        compiler_params=_params(2, blocks),
        name="delta_units",
    )(*args)


def _head_major(tok, nb, H, lane0):
    T = tok.shape[0]
    x = tok[:, lane0:lane0 + H].reshape(nb, T // nb // CHUNK, CHUNK, H)
    return jnp.transpose(x, (0, 3, 1, 2))


def _tail_rows(state, new, n):
    L = new.shape[1]
    if L >= n:
        return new[:, L - n:]
    return jnp.concatenate([state[:, state.shape[1] - (n - L):], new], axis=1)


def _layer(x, p, pool_state, conv_state, delta_state, start, lw, cfg):
    (w_in_t, w_tail_b, w_pool_b, pool_scale, w_conv8, alog_row, dtb_row, o_norm_g, w_out, ln1_g, ln1_b,
     w_gate_up, w_down_b, ln2_g, ln2_b, w_ple_gate, w_ple_proj_b, alpha) = lw
    B, L, D = x.shape
    T = B * L
    P, DN, H, d_ff = cfg["P"], cfg["DN"], cfg["H"], cfg["d_ff"]
    n_main = P + 4 * DN
    tm = _pick(T, (1024, 512, 256, 128))
    tm_big = _pick(T, (2048, 1024, 512, 256, 128))
    tm_ln = min(tm, 512)
    tm_down = min(tm, 512)
    tn_ff = _pick(d_ff, (256, 128))
    x2 = x.reshape(T, D)
    unit = math.gcd(L, CHUNK)
    hb = _pick(H, (8, 4, 2, 1))
    xb, beta_tok, gc_tok, cf = _gates(x2, w_tail_b, alog_row, dtb_row, unit=unit, tm=min(tm, 512), H=H, hb=hb)

    proj = _matmul_nt(xb, w_in_t, n_main, tm=tm_big, tn=_pick(n_main, (256, 128)), out_dtype=F32)
    proj3 = proj.reshape(B, L, n_main)
    n_pool, n_conv = pool_state.shape[1], conv_state.shape[1]
    new_pool = _tail_rows(pool_state, proj3[:, :, :P], n_pool)
    new_conv = _tail_rows(conv_state, proj3[:, :, P:P + 3 * DN], n_conv)

    pool_halo = jnp.pad(pool_state, ((0, 0), (POOL_HALO - n_pool, 0), (0, 0)))
    bb_pool = 1 if L >= 256 else _pick(B, (16, 8, 4, 2, 1))
    y_pool = _pool_mixer(proj, pool_halo, w_pool_b, pool_scale, bb=bb_pool, L=L, start=start)

    conv_halo = jnp.pad(conv_state, ((0, 0), (CONV_HALO - n_conv, 0), (0, 0)))
    offs = (P, P + DN, P + 2 * DN, P + 3 * DN)
    if unit == CHUNK:
        assert delta_state is None
        o, s_new = _delta_seq(proj, offs, conv_halo, w_conv8, _head_major(gc_tok, B, H, H),
                              _head_major(beta_tok, B, H, 0), cf,
                              o_norm_g, hb=hb, lt=_pick(L, (512, 256, 128, 64)), unroll=2)
    else:
        assert unit == CONV_HALO and T % CHUNK == 0
        nb = T // CHUNK
        o, s_new = _delta_units(proj, offs, conv_halo.reshape(B * CONV_HALO, 3 * DN), w_conv8,
                                _head_major(gc_tok, nb, H, H), _head_major(beta_tok, nb, H, 0),
                                cf, delta_state, o_norm_g, unit=unit, hb=hb)

    s1 = _matmul_resid((y_pool, o), w_out, x2, alpha, tm=tm_big, tn=tn_ff)
    h1b, h1_ln = _layer_norm(s1, ln1_g, ln1_b, tm=tm_ln)
    act = _swiglu(h1b, w_gate_up, d_ff, tm=tm_big, tn=tn_ff)
    s2 = _matmul_resid((act,), w_down_b, h1_ln, alpha, tm=tm_down, tn=_pick(D, (512, 256, 128)))
    h2b, h2_ln = _layer_norm(s2, ln2_g, ln2_b, tm=tm_ln)
    y = _ple(h2b, h2_ln, p.reshape(T, -1).astype(BF16), w_ple_gate, w_ple_proj_b, tm=tm_big, tn=tn_ff)
    return y.reshape(B, L, D), new_pool, new_conv, s_new


def kernel(x_prompt, x_sample, state_pool, state_conv, state_delta, p_prompt, p_sample, w_in, w_pool, pool_scale, w_conv, a_log, dt_bias, o_norm_g, w_out, ln1_g, ln1_b, w_gate_up, w_down, ln2_g, ln2_b, w_ple_gate, w_ple_proj):
    depth = w_in.shape[0]
    B = x_prompt.shape[0]
    D = x_prompt.shape[2]
    P = w_pool.shape[1] * w_pool.shape[2]
    H = a_log.shape[1]
    DN = H * LANE
    n_main = P + 4 * DN
    cfg = dict(P=P, DN=DN, H=H, d_ff=w_down.shape[1])
    alpha = (2.0 * depth) ** 0.25
    assert w_in.shape[2] == n_main + 2 * H and H <= CF_GC - CF_BETA and n_main % LANE == 0
    assert w_conv.shape[1] == 4 and D == P + DN and max(POOL_WINDOWS) - 1 == state_pool.shape[2]
    assert o_norm_g.shape[1] == LANE

    yp, ys = x_prompt, x_sample
    outs = [[] for _ in range(6)]
    for li in range(depth):
        w_in_t = jnp.swapaxes(w_in[li], 0, 1)
        lw = (w_in_t,
              jnp.pad(w_in_t[n_main:], ((0, LANE - 2 * H), (0, 0))).astype(BF16),
              w_pool[li].astype(BF16), pool_scale[li],
              jnp.pad(w_conv[li], ((0, CONV_HALO - w_conv.shape[1]), (0, 0))),
              jnp.pad(a_log[li], (H, LANE - 2 * H)).reshape(1, LANE),
              jnp.pad(dt_bias[li], (H, LANE - 2 * H)).reshape(1, LANE),
              o_norm_g[li], w_out[li], ln1_g[li], ln1_b[li], w_gate_up[li],
              w_down[li].astype(BF16), ln2_g[li], ln2_b[li], w_ple_gate[li],
              w_ple_proj[li].astype(BF16), alpha)
        zeros_pool = jnp.zeros((B,) + state_pool.shape[2:], F32)
        zeros_conv = jnp.zeros((B,) + state_conv.shape[2:], F32)
        yp, npl, ncv, nst = _layer(yp, p_prompt[li], zeros_pool, zeros_conv, None, 0, lw, cfg)
        ys, spl, scv, sst = _layer(ys, p_sample[li], state_pool[li], state_conv[li], state_delta[li], PAST_LEN,
                                   lw, cfg)
        for acc, val in zip(outs, (npl, ncv, nst, spl, scv, sst)):
            acc.append(val)
    return (yp, ys) + tuple(jnp.stack(o) for o in outs)
```
